```python
import jax, jax.numpy as jnp
from jax import lax
import numpy as np

D_MODEL = 2048
BATCH = 4
SEQ = 2048
DEPTH = 4
DEC_BATCH = 8
DEC_SEQ = 4
PAST_LEN = 16384
PAGE_SIZE = 128

N_MIXERS = 2
N_ATTN = (DEPTH + 1) // 2
N_CONV = DEPTH // 2
N_HEADS = 16
HEAD_DIM = D_MODEL // N_HEADS
N_KV_HEADS = 4
ROT_DIM = HEAD_DIM // 4
ROPE_THETA = 500000.0
N_IDX_HEADS = 16
IDX_DIM = 64
IDX_ROT_DIM = IDX_DIM // 4
IDX_W_SCALE = (N_IDX_HEADS * IDX_DIM) ** -0.5
TOPK_MAX = 256
Q_BLOCK = 128
CONV_WIDTH = 31
D_FF = 5632
FFN_CONV_WIDTH = 3
LN_EPS = 1e-5
DN_ALPHA = (2 * DEPTH) ** 0.25
DN_BETA = (8 * DEPTH) ** -0.25

Q_COLS = N_HEADS * HEAD_DIM
KV_COLS = N_KV_HEADS * HEAD_DIM
QI_COLS = N_IDX_HEADS * IDX_DIM
IN_COLS = Q_COLS + 2 * KV_COLS + QI_COLS + IDX_DIM + N_IDX_HEADS

kernel_name = 'dsa_conformer_convffn_deepnorm_step'


def layer_norm(x, g, b):
    xf = x.astype(jnp.float32)
    mu = jnp.mean(xf, axis=-1, keepdims=True)
    var = jnp.mean(jnp.square(xf - mu), axis=-1, keepdims=True)
    return ((xf - mu) * lax.rsqrt(var + LN_EPS) * g.astype(jnp.float32) + b.astype(jnp.float32)).astype(x.dtype)


def rope_angles(pos, rot_dim):
    inv = ROPE_THETA ** (-jnp.arange(0, rot_dim, 2, dtype=jnp.float32) / rot_dim)
    ang = pos.astype(jnp.float32)[:, None] * inv[None, :]
    return jnp.cos(ang), jnp.sin(ang)


def apply_partial_rope(x, cos, sin):
    half = cos.shape[-1]
    x1 = x[..., :half].astype(jnp.float32)
    x2 = x[..., half:2 * half].astype(jnp.float32)
    r1 = (x1 * cos - x2 * sin).astype(x.dtype)
    r2 = (x2 * cos + x1 * sin).astype(x.dtype)
    return jnp.concatenate([r1, r2, x[..., 2 * half:]], axis=-1)


def project_attn(x, w_in):
    p = x @ w_in
    lead = x.shape[:-1]
    o1 = Q_COLS
    o2 = o1 + KV_COLS
    o3 = o2 + KV_COLS
    o4 = o3 + QI_COLS
    o5 = o4 + IDX_DIM
    q = p[..., :o1].reshape(*lead, N_HEADS, HEAD_DIM)
    k = p[..., o1:o2].reshape(*lead, N_KV_HEADS, HEAD_DIM)
    v = p[..., o2:o3].reshape(*lead, N_KV_HEADS, HEAD_DIM)
    qi = p[..., o3:o4].reshape(*lead, N_IDX_HEADS, IDX_DIM)
    ki = p[..., o4:o5]
    wi = p[..., o5:]
    return q, k, v, qi, ki, wi


def rope_all(q, k, qi, ki, pos):
    c, s = rope_angles(pos, ROT_DIM)
    ci, si = rope_angles(pos, IDX_ROT_DIM)
    return (apply_partial_rope(q, c[:, None], s[:, None]),
            apply_partial_rope(k, c[:, None], s[:, None]),
            apply_partial_rope(qi, ci[:, None], si[:, None]),
            apply_partial_rope(ki, ci, si))


def index_select(qi, wi, ki, q_pos, key_pos, k_sel):
    dots = jnp.einsum('thd,sd->ths', qi, ki, preferred_element_type=jnp.float32)
    score = jnp.einsum('th,ths->ts', wi.astype(jnp.float32) * IDX_W_SCALE, jax.nn.relu(dots))
    causal = key_pos[None, :] <= q_pos[:, None]
    _, idx = lax.top_k(jnp.where(causal, score, -jnp.inf), k_sel)
    valid = key_pos[idx] <= q_pos[:, None]
    return idx, valid


def attend_selected(q, k_sel, v_sel, valid):
    tq = q.shape[0]
    qg = q.reshape(tq, N_KV_HEADS, N_HEADS // N_KV_HEADS, HEAD_DIM)
    s = jnp.einsum('tngd,tsnd->tngs', qg, k_sel, preferred_element_type=jnp.float32) * (HEAD_DIM ** -0.5)
    s = jnp.where(valid[:, None, None, :], s, -jnp.inf)
    p = jax.nn.softmax(s, axis=-1)
    o = jnp.einsum('tngs,tsnd->tngd', p.astype(v_sel.dtype), v_sel)
    return o.reshape(tq, N_HEADS * HEAD_DIM)


def attn_layer_prompt(x, w_in, w_out):
    B, T, _ = x.shape
    q, k, v, qi, ki, wi = project_attn(x, w_in)
    pos = jnp.arange(T, dtype=jnp.int32)
    q, k, qi, ki = rope_all(q, k, qi, ki, pos)
    k_sel = min(TOPK_MAX, T // 4)
    nb = T // Q_BLOCK
    qb = q.reshape(B * nb, Q_BLOCK, N_HEADS, HEAD_DIM)
    qib = qi.reshape(B * nb, Q_BLOCK, N_IDX_HEADS, IDX_DIM)
    wib = wi.reshape(B * nb, Q_BLOCK, N_IDX_HEADS)
    b_ids = jnp.repeat(jnp.arange(B, dtype=jnp.int32), nb)
    blk_ids = jnp.tile(jnp.arange(nb, dtype=jnp.int32), B)

    def one_block(args):
        q_blk, qi_blk, wi_blk, b, blk = args
        q_pos = blk * Q_BLOCK + jnp.arange(Q_BLOCK, dtype=jnp.int32)
        idx, valid = index_select(qi_blk, wi_blk, ki[b], q_pos, pos, k_sel)
        return attend_selected(q_blk, k[b][idx], v[b][idx], valid)

    o = lax.map(one_block, (qb, qib, wib, b_ids, blk_ids))
    y = o.reshape(B, T, Q_COLS) @ w_out
    return y, k, v, ki


def attn_layer_sample(x, cache_k, cache_v, cache_ki, page_table, w_in, w_out):
    Bd, Tn, _ = x.shape
    q, k, v, qi, ki, wi = project_attn(x, w_in)
    pos_new = PAST_LEN + jnp.arange(Tn, dtype=jnp.int32)
    q, k, qi, ki = rope_all(q, k, qi, ki, pos_new)
    ki_past = cache_ki[page_table].reshape(Bd, PAST_LEN, IDX_DIM)
    ki_all = jnp.concatenate([ki_past, ki], axis=1)
    key_pos = jnp.arange(PAST_LEN + Tn, dtype=jnp.int32)
    k_sel = min(TOPK_MAX, (PAST_LEN + Tn) // 4)

    def one_seq(q_s, qi_s, wi_s, ki_s, k_new, v_new, pages):
        idx, valid = index_select(qi_s, wi_s, ki_s, pos_new, key_pos, k_sel)
        in_past = (idx < PAST_LEN)[..., None, None]
        past_idx = jnp.minimum(idx, PAST_LEN - 1)
        phys = pages[past_idx // PAGE_SIZE]
        off = past_idx % PAGE_SIZE
        new_idx = jnp.clip(idx - PAST_LEN, 0, Tn - 1)
        k_s = jnp.where(in_past, cache_k[phys, off], k_new[new_idx])
        v_s = jnp.where(in_past, cache_v[phys, off], v_new[new_idx])
        return attend_selected(q_s, k_s, v_s, valid)

    o = jax.vmap(one_seq)(q, qi, wi, ki_all, k, v, page_table)
    y = o.reshape(Bd, Tn, Q_COLS) @ w_out
    return y, k, v, ki


def causal_dwconv(x_pad, w, b):
    c = x_pad.shape[-1]
    y = lax.conv_general_dilated(x_pad, w[:, None, :], window_strides=(1,), padding='VALID',
                                 dimension_numbers=('NWC', 'WIO', 'NWC'), feature_group_count=c)
    return y + b


def conformer_conv(x, hist, w_pw1, b_pw1, w_dw, b_dw, ln_g, ln_b, w_pw2, b_pw2):
    a, gate = jnp.split(x @ w_pw1 + b_pw1, 2, axis=-1)
    a = a * jax.nn.sigmoid(gate)
    a_pad = jnp.concatenate([hist, a], axis=1)
    h = jax.nn.silu(layer_norm(causal_dwconv(a_pad, w_dw, b_dw), ln_g, ln_b))
    return h @ w_pw2 + b_pw2, a_pad[:, -(CONV_WIDTH - 1):]


def conv_ffn(x, hist, w_gate, w_up, w_c, b_c, w_down):
    g = x @ w_gate
    g_pad = jnp.concatenate([hist, g], axis=1)
    h = jax.nn.silu(causal_dwconv(g_pad, w_c, b_c)) * (x @ w_up)
    return h @ w_down, g_pad[:, -(FFN_CONV_WIDTH - 1):]


def run_trunk(x, attn_fn, conv_hist, ffn_hist, w_pw1, b_pw1, w_dw, b_dw, ln_conv_g, ln_conv_b,
              w_pw2, b_pw2, w_ffn_gate, w_ffn_up, w_ffn_conv, b_ffn_conv, w_ffn_down,
              ln_mix_g, ln_mix_b, ln_ffn_g, ln_ffn_b):
    ks, vs, kis, convs, ffns = [], [], [], [], []
    for i in range(DEPTH):
        j = i // N_MIXERS
        if i % N_MIXERS == 0:
            m, k, v, ki = attn_fn(j, x)
            ks.append(k)
            vs.append(v)
            kis.append(ki)
        else:
            m, c_state = conformer_conv(x, conv_hist[j], w_pw1[j], b_pw1[j], w_dw[j], b_dw[j],
                                        ln_conv_g[j], ln_conv_b[j], w_pw2[j], b_pw2[j])
            convs.append(c_state)
        x = layer_norm(DN_ALPHA * x + m, ln_mix_g[i], ln_mix_b[i])
        f, g_state = conv_ffn(x, ffn_hist[i], w_ffn_gate[i], w_ffn_up[i], w_ffn_conv[i],
                              b_ffn_conv[i], w_ffn_down[i])
        ffns.append(g_state)
        x = layer_norm(DN_ALPHA * x + f, ln_ffn_g[i], ln_ffn_b[i])
    return x, jnp.stack(ks), jnp.stack(vs), jnp.stack(kis), jnp.stack(convs), jnp.stack(ffns)


def setup_inputs(seed: int = 0) -> dict:
    key = jax.random.key(seed)
    keys = iter(jax.random.split(key, 40))

    def nrm(shape, scale):
        return jax.random.normal(next(keys), shape, jnp.float32) * scale

    n_pages = PAST_LEN // PAGE_SIZE
    n_used = DEC_BATCH * n_pages
    n_pool = n_used + (n_used + 3) // 4
    perm = jax.random.permutation(next(keys), n_pool)
    page_table = perm[:n_used].reshape(DEC_BATCH, n_pages).astype(jnp.int32)

    col_scale = jnp.concatenate([jnp.ones((Q_COLS + KV_COLS,), jnp.float32),
                                 jnp.full((KV_COLS,), DN_BETA, jnp.float32),
                                 jnp.ones((QI_COLS + IDX_DIM + N_IDX_HEADS,), jnp.float32)])
    d = D_MODEL
    return {
        'x_prompt': nrm((BATCH, SEQ, d), 1.0),
        'x_sample': nrm((DEC_BATCH, DEC_SEQ, d), 1.0),
        'cache_k': nrm((N_ATTN, n_pool, PAGE_SIZE, N_KV_HEADS, HEAD_DIM), 1.0),
        'cache_v': nrm((N_ATTN, n_pool, PAGE_SIZE, N_KV_HEADS, HEAD_DIM), DN_BETA),
        'cache_kidx': nrm((N_ATTN, n_pool, PAGE_SIZE, IDX_DIM), 1.0),
        'state_conv': nrm((N_CONV, DEC_BATCH, CONV_WIDTH - 1, d), 0.5),
        'state_ffn': nrm((DEPTH, DEC_BATCH, FFN_CONV_WIDTH - 1, D_FF), 1.0),
        'page_table': page_table,
        'w_attn_in': nrm((N_ATTN, d, IN_COLS), d ** -0.5) * col_scale,
        'w_attn_out': nrm((N_ATTN, Q_COLS, d), DN_BETA * Q_COLS ** -0.5),
        'w_pw1': nrm((N_CONV, d, 2 * d), d ** -0.5),
        'b_pw1': nrm((N_CONV, 2 * d), 0.01),
        'w_dw': nrm((N_CONV, CONV_WIDTH, d), CONV_WIDTH ** -0.5),
        'b_dw': nrm((N_CONV, d), 0.01),
        'ln_conv_g': 1.0 + nrm((N_CONV, d), 0.01),
        'ln_conv_b': nrm((N_CONV, d), 0.01),
        'w_pw2': nrm((N_CONV, d, d), DN_BETA * d ** -0.5),
        'b_pw2': nrm((N_CONV, d), 0.01),
        'w_ffn_gate': nrm((DEPTH, d, D_FF), d ** -0.5),
        'w_ffn_up': nrm((DEPTH, d, D_FF), d ** -0.5),
        'w_ffn_conv': nrm((DEPTH, FFN_CONV_WIDTH, D_FF), FFN_CONV_WIDTH ** -0.5),
        'b_ffn_conv': nrm((DEPTH, D_FF), 0.01),
        'w_ffn_down': nrm((DEPTH, D_FF, d), DN_BETA * D_FF ** -0.5),
        'ln_mix_g': 1.0 + nrm((DEPTH, d), 0.01),
        'ln_mix_b': nrm((DEPTH, d), 0.01),
        'ln_ffn_g': 1.0 + nrm((DEPTH, d), 0.01),
        'ln_ffn_b': nrm((DEPTH, d), 0.01),
    }


def reference(x_prompt, x_sample, cache_k, cache_v, cache_kidx, state_conv, state_ffn, page_table,
              w_attn_in, w_attn_out, w_pw1, b_pw1, w_dw, b_dw, ln_conv_g, ln_conv_b, w_pw2, b_pw2,
              w_ffn_gate, w_ffn_up, w_ffn_conv, b_ffn_conv, w_ffn_down,
              ln_mix_g, ln_mix_b, ln_ffn_g, ln_ffn_b):
    shared = (w_pw1, b_pw1, w_dw, b_dw, ln_conv_g, ln_conv_b, w_pw2, b_pw2,
              w_ffn_gate, w_ffn_up, w_ffn_conv, b_ffn_conv, w_ffn_down,
              ln_mix_g, ln_mix_b, ln_ffn_g, ln_ffn_b)

    bp = x_prompt.shape[0]
    conv_zero = jnp.zeros((N_CONV, bp, CONV_WIDTH - 1, D_MODEL), x_prompt.dtype)
    ffn_zero = jnp.zeros((DEPTH, bp, FFN_CONV_WIDTH - 1, D_FF), x_prompt.dtype)
    y_prompt, k_p, v_p, ki_p, conv_p, ffn_p = run_trunk(
        x_prompt, lambda j, h: attn_layer_prompt(h, w_attn_in[j], w_attn_out[j]),
        conv_zero, ffn_zero, *shared)

    y_sample, k_s, v_s, ki_s, conv_s, ffn_s = run_trunk(
        x_sample, lambda j, h: attn_layer_sample(h, cache_k[j], cache_v[j], cache_kidx[j], page_table,
                                                 w_attn_in[j], w_attn_out[j]),
        state_conv, state_ffn, *shared)

    return (y_prompt, y_sample, k_p, v_p, ki_p, conv_p, ffn_p, k_s, v_s, ki_s, conv_s, ffn_s)
```

```python
import functools

import jax
import jax.numpy as jnp
from jax import lax
from jax.experimental import pallas as pl
from jax.experimental.pallas import tpu as pltpu

F32 = jnp.float32
BF16 = jnp.bfloat16
I32 = jnp.int32

N_HEADS = 16
N_KV_HEADS = 4
HEAD_DIM = 128
GROUP = N_HEADS // N_KV_HEADS
ROT_DIM = HEAD_DIM // 4
ROPE_THETA = 500000.0
N_IDX_HEADS = 16
IDX_DIM = 64
IDX_ROT_DIM = IDX_DIM // 4
IDX_W_SCALE = (N_IDX_HEADS * IDX_DIM) ** -0.5
TOPK_MAX = 256
Q_BLOCK = 128
LN_EPS = 1e-5
LANES = 128
SUBLANES = 8
PAGES_PER_STEP = 8
T_PAD = 8
NEG_BIG = -1e30
INT_MIN = -(2 ** 31)
INT_MAX = 2 ** 31 - 1
NEG_INF_KEY = INT_MIN + 0x007FFFFF
VMEM_LIMIT = 56 * 2 ** 20


def _params(*sem):
    return pltpu.CompilerParams(dimension_semantics=sem, vmem_limit_bytes=VMEM_LIMIT)


def _dot(a, b):
    return jnp.dot(a, b, preferred_element_type=F32)


def _dot_nt(a, b):
    return lax.dot_general(a, b, (((1,), (1,)), ((), ())), preferred_element_type=F32)


def _layer_norm(z, g, b):
    mu = jnp.mean(z, axis=-1, keepdims=True)
    zc = z - mu
    var = jnp.mean(zc * zc, axis=-1, keepdims=True)
    return zc * lax.rsqrt(var + LN_EPS) * g + b


def _lane_tile(x, rep):
    return x if rep == 1 else jnp.concatenate([x] * rep, axis=1)


def _proj_body(*refs, shift):
    x_ref, w_ref = refs[0], refs[1]
    if shift:
        c_ref, s1_ref, s2_ref = refs[2:5]
        outs = refs[5:]
    else:
        outs = refs[2:]
    y = _dot(x_ref[...], w_ref[...].astype(BF16))
    if shift:
        tn = y.shape[1]
        rep = tn // LANES
        y = (y * _lane_tile(c_ref[...], rep)
             + pltpu.roll(y, shift, 1) * _lane_tile(s1_ref[...], rep)
             + pltpu.roll(y, tn - shift, 1) * _lane_tile(s2_ref[...], rep))
    for o in outs:
        o[...] = y.astype(o.dtype)


def _proj(x_bf, w, layer, col_off, ncols, tabs, shift, out_dtypes, tm, tn):
    M, K = x_bf.shape
    assert col_off % tn == 0 and ncols % tn == 0 and M % tm == 0
    cb = col_off // tn
    in_specs = [pl.BlockSpec((tm, K), lambda i, j: (i, 0)),
                pl.BlockSpec((None, K, tn), lambda i, j: (layer, 0, cb + j))]
    args = [x_bf, w]
    if shift:
        period = tabs[0].shape[0] // tm
        for t in tabs:
            in_specs.append(pl.BlockSpec((tm, LANES), lambda i, j: (i % period, 0)))
            args.append(t)
    outs = pl.pallas_call(
        functools.partial(_proj_body, shift=shift),
        grid=(M // tm, ncols // tn),
        in_specs=in_specs,
        out_specs=[pl.BlockSpec((tm, tn), lambda i, j: (i, j)) for _ in out_dtypes],
        out_shape=[jax.ShapeDtypeStruct((M, ncols), dt) for dt in out_dtypes],
        compiler_params=_params("parallel", "arbitrary"),
        name="proj",
    )(*args)
    return outs


def _glu_body(x_ref, wa_ref, wg_ref, ba_ref, bg_ref, a_ref):
    x = x_ref[...]
    a = _dot(x, wa_ref[...].astype(BF16)) + ba_ref[...]
    g = _dot(x, wg_ref[...].astype(BF16)) + bg_ref[...]
    a_ref[...] = a * jax.nn.sigmoid(g)


def _glu(x_bf, w_pw1, b_pw1, layer, tm, tn):
    M, K = x_bf.shape
    C = w_pw1.shape[2] // 2
    nb = C // tn
    b2 = b_pw1[layer].reshape(1, 2 * C)
    return pl.pallas_call(
        _glu_body,
        grid=(M // tm, nb),
        in_specs=[pl.BlockSpec((tm, K), lambda i, j: (i, 0)),
                  pl.BlockSpec((None, K, tn), lambda i, j: (layer, 0, j)),
                  pl.BlockSpec((None, K, tn), lambda i, j: (layer, 0, nb + j)),
                  pl.BlockSpec((1, tn), lambda i, j: (0, j)),
                  pl.BlockSpec((1, tn), lambda i, j: (0, nb + j))],
        out_specs=pl.BlockSpec((tm, tn), lambda i, j: (i, j)),
        out_shape=jax.ShapeDtypeStruct((M, C), F32),
        compiler_params=_params("parallel", "arbitrary"),
        name="glu",
    )(x_bf, w_pw1, w_pw1, b2, b2)


def _dwconv_body(a_ref, hist_ref, w_ref, b_ref, g_ref, beta_ref, h_ref, ext_ref, conv_ref,
                 *, stride, hist_rows, width, seq_tiles, row_block):
    tm, C = a_ref.shape
    hist = hist_ref[...]
    if seq_tiles:
        hist = jnp.where(pl.program_id(0) % seq_tiles == 0, 0.0, hist)
    ext_ref[0:hist_rows, :] = hist
    ext_ref[hist_rows:hist_rows + tm, :] = a_ref[...]

    def chunk(ci, carry):
        c0 = pl.multiple_of(ci * LANES, LANES)
        wch = w_ref[:, pl.ds(c0, LANES)]
        bch = b_ref[:, pl.ds(c0, LANES)]
        for r0 in range(0, tm, row_block):
            acc = jnp.broadcast_to(bch, (row_block, LANES))
            for k in range(width):
                off = hist_rows - (width - 1 - k) * stride + r0
                acc = acc + ext_ref[pl.ds(off, row_block), pl.ds(c0, LANES)] * wch[k:k + 1, :]
            conv_ref[pl.ds(r0, row_block), pl.ds(c0, LANES)] = acc
        return carry

    lax.fori_loop(0, C // LANES, chunk, 0)
    y = _layer_norm(conv_ref[...], g_ref[...], beta_ref[...])
    h_ref[...] = (y * jax.nn.sigmoid(y)).astype(h_ref.dtype)


def _dwconv(a, hist, w_dw, b_dw, ln_g, ln_b, *, tm, stride, hist_rows, seq_tiles):
    M, C = a.shape
    width = w_dw.shape[0]
    if seq_tiles:
        per = tm // hist_rows
        hist_arr = a
        hist_spec = pl.BlockSpec((hist_rows, C), lambda i: (jnp.maximum(i * per - 1, 0), 0))
    else:
        hist_arr = hist
        hist_spec = pl.BlockSpec((hist_rows, C), lambda i: (0, 0))
    vec = lambda: pl.BlockSpec((1, C), lambda i: (0, 0))
    return pl.pallas_call(
        functools.partial(_dwconv_body, stride=stride, hist_rows=hist_rows, width=width,
                          seq_tiles=seq_tiles, row_block=min(tm, 64)),
        grid=(M // tm,),
        in_specs=[pl.BlockSpec((tm, C), lambda i: (i, 0)), hist_spec,
                  pl.BlockSpec((width, C), lambda i: (0, 0)), vec(), vec(), vec()],
        out_specs=pl.BlockSpec((tm, C), lambda i: (i, 0)),
        out_shape=jax.ShapeDtypeStruct((M, C), BF16),
        scratch_shapes=[pltpu.VMEM((hist_rows + tm, C), F32), pltpu.VMEM((tm, C), F32)],
        compiler_params=_params("arbitrary"),
        name="dwconv",
    )(a, hist_arr, w_dw, b_dw.reshape(1, C), ln_g.reshape(1, C), ln_b.reshape(1, C))


def _mm_res_ln_body(a_ref, w_ref, bias_ref, res_ref, g_ref, b_ref, xo_ref, xb_ref, *, nj, tn, alpha):
    j = pl.program_id(1)
    y = _dot(a_ref[...], w_ref[...].astype(BF16))
    xo_ref[:, pl.ds(pl.multiple_of(j * tn, tn), tn)] = y

    @pl.when(j == nj - 1)
    def _():
        z = alpha * res_ref[...] + (xo_ref[...] + bias_ref[...])
        out = _layer_norm(z, g_ref[...], b_ref[...])
        xo_ref[...] = out
        xb_ref[...] = out.astype(BF16)


def _mm_res_ln(a_bf, w, layer, bias, res, ln_g, ln_b, alpha, tm, tn):
    M, K = a_bf.shape
    N = w.shape[2]
    nj = N // tn
    vec = lambda: pl.BlockSpec((1, N), lambda i, j: (0, 0))
    return pl.pallas_call(
        functools.partial(_mm_res_ln_body, nj=nj, tn=tn, alpha=alpha),
        grid=(M // tm, nj),
        in_specs=[pl.BlockSpec((tm, K), lambda i, j: (i, 0)),
                  pl.BlockSpec((None, K, tn), lambda i, j: (layer, 0, j)),
                  vec(),
                  pl.BlockSpec((tm, N), lambda i, j: (i, 0)),
                  vec(), vec()],
        out_specs=[pl.BlockSpec((tm, N), lambda i, j: (i, 0)),
                   pl.BlockSpec((tm, N), lambda i, j: (i, 0))],
        out_shape=[jax.ShapeDtypeStruct((M, N), F32), jax.ShapeDtypeStruct((M, N), BF16)],
        compiler_params=_params("parallel", "arbitrary"),
        name="mm_res_ln",
    )(a_bf, w, bias.reshape(1, N), res, ln_g.reshape(1, N), ln_b.reshape(1, N))


def _ffn_up_body(x_ref, wg_ref, wu_ref, hist_ref, wc_ref, bc_ref, h_ref, gt_ref, ext_ref,
                 *, stride, hist_rows):
    x = x_ref[...]
    tm = x.shape[0]
    H = hist_rows
    g = _dot(x, wg_ref[...].astype(BF16))
    ext_ref[0:H, :] = hist_ref[...]
    ext_ref[H:H + tm, :] = g
    gt_ref[...] = g[tm - H:tm, :]
    u = _dot(x, wu_ref[...].astype(BF16))
    wc = wc_ref[...]
    c = (bc_ref[...] + wc[2:3, :] * g
         + wc[1:2, :] * ext_ref[H - stride:H - stride + tm, :]
         + wc[0:1, :] * ext_ref[H - 2 * stride:H - 2 * stride + tm, :])
    h_ref[...] = (c * jax.nn.sigmoid(c) * u).astype(h_ref.dtype)


def _ffn_up(x_bf, w_gate, w_up, layer, hist, w_c, b_c, *, tm, tf, stride, hist_rows):
    M, K = x_bf.shape
    F = w_gate.shape[2]
    H = hist_rows
    return pl.pallas_call(
        functools.partial(_ffn_up_body, stride=stride, hist_rows=H),
        grid=(M // tm, F // tf),
        in_specs=[pl.BlockSpec((tm, K), lambda i, j: (i, 0)),
                  pl.BlockSpec((None, K, tf), lambda i, j: (layer, 0, j)),
                  pl.BlockSpec((None, K, tf), lambda i, j: (layer, 0, j)),
                  pl.BlockSpec((H, tf), lambda i, j: (i, j)),
                  pl.BlockSpec((3, tf), lambda i, j: (0, j)),
                  pl.BlockSpec((1, tf), lambda i, j: (0, j))],
        out_specs=[pl.BlockSpec((tm, tf), lambda i, j: (i, j)),
                   pl.BlockSpec((H, tf), lambda i, j: (i, j))],
        out_shape=[jax.ShapeDtypeStruct((M, F), BF16),
                   jax.ShapeDtypeStruct((M // tm * H, F), F32)],
        scratch_shapes=[pltpu.VMEM((H + tm, tf), F32)],
        compiler_params=_params("parallel", "arbitrary"),
        name="ffn_up",
    )(x_bf, w_gate, w_up, hist, w_c, b_c.reshape(1, F))


def _order_key(score):
    bits = pltpu.bitcast(score + 0.0, I32)
    return jnp.where(bits < 0, bits ^ INT_MAX, bits)


def _count(mask):
    return jnp.sum(jnp.where(mask, 1, 0).astype(I32), axis=0, keepdims=True)


def _select_topk(key_ref, cut_ref, k, lane_ok):
    S, L = key_ref.shape
    cnt0 = _count(key_ref[...] >= 0)
    base = jnp.where(cnt0 >= k, 0, INT_MIN).astype(I32)

    def bit_step(t, base):
        cand = base | lax.shift_left(jnp.int32(1), 30 - t)
        return jnp.where(_count(key_ref[...] >= cand) >= k, cand, base)

    thr = lax.fori_loop(0, 31, bit_step, base)
    key = key_ref[...]
    n_gt = _count(key > thr)
    n_ge = _count(key >= thr)
    need = k - n_gt
    tie = (n_ge > k) & (thr > NEG_INF_KEY) & lane_ok
    cut_ref[...] = jnp.full((1, L), INT_MAX, I32)
    n_tie = jnp.sum(jnp.where(tie, 1, 0).astype(I32))

    @pl.when(n_tie > 0)
    def _():
        nbits = max(1, (S - 1).bit_length())
        rows = lax.broadcasted_iota(I32, (S, L), 0)

        def idx_step(t, pos):
            cand = pos | lax.shift_left(jnp.int32(1), nbits - 1 - t)
            c = _count((key_ref[...] == thr) & (rows < cand))
            return jnp.where(c < need, cand, pos)

        cut_ref[...] = lax.fori_loop(0, nbits, idx_step, jnp.zeros((1, L), I32))

    return thr


def _attn_prompt_body(qi_ref, wit_ref, kiwi_ref, q_ref, k_ref, vt_ref, o_ref,
                      ki2_ref, sc_ref, key_ref, cut_ref, *, k_sel):
    S = kiwi_ref.shape[0]
    tq = qi_ref.shape[0]
    qb = pl.program_id(1)

    @pl.when(qb == 0)
    def _():
        lane = lax.broadcasted_iota(I32, (S, LANES), 1)
        lo = jnp.where(lane < IDX_DIM, kiwi_ref[...], 0.0)
        ki2_ref[0:S, :] = lo.astype(BF16)
        ki2_ref[S:2 * S, :] = pltpu.roll(lo, IDX_DIM, 1).astype(BF16)

    ws = wit_ref[...] * IDX_W_SCALE
    for p in range(N_IDX_HEADS // 2):
        d = _dot_nt(ki2_ref[...], qi_ref[:, p * LANES:(p + 1) * LANES])
        part = (ws[2 * p:2 * p + 1, :] * jnp.maximum(d[0:S, :], 0.0)
                + ws[2 * p + 1:2 * p + 2, :] * jnp.maximum(d[S:2 * S, :], 0.0))
        if p == 0:
            sc_ref[...] = part
        else:
            sc_ref[...] += part

    rows = lax.broadcasted_iota(I32, (S, tq), 0)
    qpos = qb * tq + lax.broadcasted_iota(I32, (S, tq), 1)
    causal = rows <= qpos
    key_ref[...] = _order_key(jnp.where(causal, sc_ref[...], -jnp.inf))
    thr = _select_topk(key_ref, cut_ref, k_sel, jnp.full((1, tq), True))
    key = key_ref[...]
    sel = ((key > thr) | ((key == thr) & (rows <= cut_ref[...]))) & causal
    bias = _lane_tile(jnp.where(sel, 0.0, NEG_BIG), GROUP)

    scale = HEAD_DIM ** -0.5
    for n in range(N_KV_HEADS):
        qs = jnp.concatenate([q_ref[:, (n * GROUP + g) * HEAD_DIM:(n * GROUP + g + 1) * HEAD_DIM]
                              for g in range(GROUP)], axis=0)
        s = _dot_nt(k_ref[:, n * HEAD_DIM:(n + 1) * HEAD_DIM], qs) * scale + bias
        m = jnp.max(s, axis=0, keepdims=True)
        e = jnp.exp(s - m)
        l = jnp.sum(e, axis=0, keepdims=True)
        pt = (e * (1.0 / l)).astype(BF16)
        ot = _dot(vt_ref[n * HEAD_DIM:(n + 1) * HEAD_DIM, :], pt)
        for g in range(GROUP):
            h = n * GROUP + g
            o_ref[:, h * HEAD_DIM:(h + 1) * HEAD_DIM] = (
                ot[:, g * tq:(g + 1) * tq].T.astype(o_ref.dtype))


def _attn_prompt(qi_bf, wit, kiwi, q_bf, k_bf, vt_bf, B, T, k_sel):
    tq = Q_BLOCK
    nb = T // tq
    QC = N_HEADS * HEAD_DIM
    KC = N_KV_HEADS * HEAD_DIM
    QI = N_IDX_HEADS * IDX_DIM
    return pl.pallas_call(
        functools.partial(_attn_prompt_body, k_sel=k_sel),
        grid=(B, nb),
        in_specs=[pl.BlockSpec((tq, QI), lambda b, i: (b * nb + i, 0)),
                  pl.BlockSpec((N_IDX_HEADS, tq), lambda b, i: (0, b * nb + i)),
                  pl.BlockSpec((T, LANES), lambda b, i: (b, 0)),
                  pl.BlockSpec((tq, QC), lambda b, i: (b * nb + i, 0)),
                  pl.BlockSpec((T, KC), lambda b, i: (b, 0)),
                  pl.BlockSpec((None, KC, T), lambda b, i: (b, 0, 0))],
        out_specs=pl.BlockSpec((tq, QC), lambda b, i: (b * nb + i, 0)),
        out_shape=jax.ShapeDtypeStruct((B * T, QC), BF16),
        scratch_shapes=[pltpu.VMEM((2 * T, LANES), BF16), pltpu.VMEM((T, tq), F32),
                        pltpu.VMEM((T, tq), I32), pltpu.VMEM((1, tq), I32)],
        compiler_params=_params("parallel", "arbitrary"),
        name="attn_prompt",
    )(qi_bf, wit, kiwi, q_bf, k_bf, vt_bf)


def _page_scores(ki, qit, w_lane):
    d = _dot(ki.astype(BF16), qit)
    z = jnp.maximum(d, 0.0) * w_lane
    for sh in (64, 32, 16, 8):
        z = z + pltpu.roll(z, sh, 1)
    return z


def _samp_score_body(pt_ref, *refs, n_tok):
    G = PAGES_PER_STEP
    pages = refs[:G]
    kinew_ref, qit_ref, wl_ref, out_ref = refs[G:G + 4]
    p = pl.program_id(1)
    n_steps = pl.num_programs(1)
    qit = qit_ref[...]
    wl = wl_ref[...]

    @pl.when(p < n_steps - 1)
    def _():
        for g in range(G):
            out_ref[g * LANES:(g + 1) * LANES, :] = _page_scores(pages[g][...], qit, wl)

    @pl.when(p == n_steps - 1)
    def _():
        z = _page_scores(kinew_ref[...], qit, wl)
        row = lax.broadcasted_iota(I32, (LANES, LANES), 0)
        tok = lax.broadcasted_iota(I32, (LANES, LANES), 1) % T_PAD
        ok = (row <= tok) & (row < n_tok)
        out_ref[0:LANES, :] = jnp.where(ok, z, -jnp.inf)
        out_ref[LANES:G * LANES, :] = jnp.full(((G - 1) * LANES, LANES), -jnp.inf, F32)


def _page_index_map(g, layer_off, n_steps):
    def index_map(b, p, pt):
        pg = jnp.minimum(p, n_steps - 2) * PAGES_PER_STEP + g
        return (layer_off + pt[b, pg], 0, 0)
    return index_map


def _samp_score(page_table, cache_ki, layer_off, kinew, qit, w_lane, n_tok):
    Bd, n_pages = page_table.shape
    G = PAGES_PER_STEP
    n_steps = n_pages // G + 1
    page = cache_ki.shape[1]
    in_specs = [pl.BlockSpec((None, page, IDX_DIM), _page_index_map(g, layer_off, n_steps))
                for g in range(G)]
    in_specs += [pl.BlockSpec((None, LANES, IDX_DIM), lambda b, p, pt: (b, 0, 0)),
                 pl.BlockSpec((None, IDX_DIM, LANES), lambda b, p, pt: (b, 0, 0)),
                 pl.BlockSpec((None, 1, LANES), lambda b, p, pt: (b, 0, 0))]
    return pl.pallas_call(
        functools.partial(_samp_score_body, n_tok=n_tok),
        grid_spec=pltpu.PrefetchScalarGridSpec(
            num_scalar_prefetch=1, grid=(Bd, n_steps), in_specs=in_specs,
            out_specs=pl.BlockSpec((None, G * page, LANES), lambda b, p, pt: (b, p, 0))),
        out_shape=jax.ShapeDtypeStruct((Bd, n_steps * G * page, LANES), F32),
        compiler_params=_params("parallel", "arbitrary"),
        name="samp_score",
    )(page_table, *([cache_ki] * G), kinew, qit, w_lane)


def _samp_select_body(sc_ref, thr_ref, cut_ref, key_ref, *, k_sel, n_lanes, n_tok):
    key_ref[...] = _order_key(sc_ref[...])
    lane = lax.broadcasted_iota(I32, (1, LANES), 1)
    lane_ok = (lane < n_lanes) & (lane % T_PAD < n_tok)
    thr_ref[...] = _select_topk(key_ref, cut_ref, k_sel, lane_ok)


def _samp_select(score_t, k_sel, n_lanes, n_tok):
    S = score_t.shape[0]
    return pl.pallas_call(
        functools.partial(_samp_select_body, k_sel=k_sel, n_lanes=n_lanes, n_tok=n_tok),
        out_shape=[jax.ShapeDtypeStruct((1, LANES), I32), jax.ShapeDtypeStruct((1, LANES), I32)],
        scratch_shapes=[pltpu.VMEM((S, LANES), I32)],
        compiler_params=pltpu.CompilerParams(vmem_limit_bytes=VMEM_LIMIT),
        name="samp_select",
    )(score_t)


def _samp_attend_body(pt_ref, *refs):
    G = PAGES_PER_STEP
    kp = refs[:G]
    vp = refs[G:2 * G]
    (knew_ref, vnew_ref, sc_ref, thr_ref, cut_ref, qbd_ref, o_ref,
     m_ref, l_ref, acc_ref) = refs[2 * G:]
    p = pl.program_id(1)
    n_steps = pl.num_programs(1)
    scale = HEAD_DIM ** -0.5

    @pl.when(p == 0)
    def _():
        m_ref[...] = jnp.full(m_ref.shape, NEG_BIG, F32)
        l_ref[...] = jnp.zeros(l_ref.shape, F32)
        acc_ref[...] = jnp.zeros(acc_ref.shape, F32)

    qbd = qbd_ref[...]
    thr = thr_ref[...]
    cut = cut_ref[...]

    def update(k_page, v_page, g):
        s = _dot(k_page.astype(BF16), qbd) * scale
        key = _order_key(sc_ref[g * LANES:(g + 1) * LANES, :])
        row = (p * G + g) * LANES + lax.broadcasted_iota(I32, (LANES, LANES), 0)
        sel = (key > thr) | ((key == thr) & (row <= cut))
        s = s + jnp.where(sel, 0.0, NEG_BIG)
        m_old = m_ref[...]
        m_new = jnp.maximum(m_old, jnp.max(s, axis=0, keepdims=True))
        corr = jnp.exp(m_old - m_new)
        e = jnp.exp(s - m_new)
        l_ref[...] = l_ref[...] * corr + jnp.sum(e, axis=0, keepdims=True)
        m_ref[...] = m_new
        acc_ref[...] = acc_ref[...] * corr + _dot(v_page.T.astype(BF16), e.astype(BF16))

    @pl.when(p < n_steps - 1)
    def _():
        for g in range(G):
            update(kp[g][...], vp[g][...], g)

    @pl.when(p == n_steps - 1)
    def _():
        update(knew_ref[...], vnew_ref[...], 0)
        o_ref[...] = acc_ref[...] * (1.0 / l_ref[...])


def _samp_attend(page_table, cache_k, cache_v, layer_off, knew, vnew, score_tiled, thr_l, cut_l, qbd):
    Bd, n_pages = page_table.shape
    G = PAGES_PER_STEP
    n_steps = n_pages // G + 1
    page = cache_k.shape[1]
    KC = N_KV_HEADS * HEAD_DIM
    in_specs = [pl.BlockSpec((None, page, KC), _page_index_map(g, layer_off, n_steps)) for g in range(G)]
    in_specs += [pl.BlockSpec((None, page, KC), _page_index_map(g, layer_off, n_steps)) for g in range(G)]
    per_seq = lambda r, c: pl.BlockSpec((None, r, c), lambda b, p, pt: (b, 0, 0))
    in_specs += [per_seq(LANES, KC), per_seq(LANES, KC),
                 pl.BlockSpec((None, G * page, LANES), lambda b, p, pt: (b, p, 0)),
                 per_seq(1, LANES), per_seq(1, LANES), per_seq(KC, LANES)]
    return pl.pallas_call(
        _samp_attend_body,
        grid_spec=pltpu.PrefetchScalarGridSpec(
            num_scalar_prefetch=1, grid=(Bd, n_steps), in_specs=in_specs,
            out_specs=pl.BlockSpec((None, KC, LANES), lambda b, p, pt: (b, 0, 0)),
            scratch_shapes=[pltpu.VMEM((1, LANES), F32), pltpu.VMEM((1, LANES), F32),
                            pltpu.VMEM((KC, LANES), F32)]),
        out_shape=jax.ShapeDtypeStruct((Bd, KC, LANES), F32),
        compiler_params=_params("parallel", "arbitrary"),
        name="samp_attend",
    )(page_table, *([cache_k] * G), *([cache_v] * G), knew, vnew, score_tiled, thr_l, cut_l, qbd)


def _rope_tables(pos, rot_dim, period):
    half = rot_dim // 2
    inv = ROPE_THETA ** (-jnp.arange(0, rot_dim, 2, dtype=F32) / rot_dim)
    ang = pos.astype(F32)[:, None] * inv[None, :]
    cos, sin = jnp.cos(ang), jnp.sin(ang)
    lane = jnp.arange(LANES) % period
    fi = lane % half
    c = jnp.where(lane[None, :] < rot_dim, cos[:, fi], 1.0)
    s_hi = jnp.where((lane[None, :] >= half) & (lane[None, :] < rot_dim), sin[:, fi], 0.0)
    s_lo = jnp.where(lane[None, :] < half, -sin[:, fi], 0.0)
    return c, s_hi, s_lo


class _Group:
    def __init__(self, rows, seq_rows, stride, pos, tm, tm_down):
        self.rows = rows
        self.seq_rows = seq_rows
        self.stride = stride
        self.tm = tm
        self.tm_down = tm_down
        self.tab_q = _rope_tables(pos, ROT_DIM, HEAD_DIM)
        self.tab_i = _rope_tables(pos, IDX_ROT_DIM, IDX_DIM)


def _in_proj(grp, x_bf, w_in, j):
    QC = N_HEADS * HEAD_DIM
    KC = N_KV_HEADS * HEAD_DIM
    QI = N_IDX_HEADS * IDX_DIM
    tm = grp.tm
    (q_bf,) = _proj(x_bf, w_in, j, 0, QC, grp.tab_q, ROT_DIM // 2, [BF16], tm, 512)
    k_f, k_bf = _proj(x_bf, w_in, j, QC, KC, grp.tab_q, ROT_DIM // 2, [F32, BF16], tm, 512)
    v_f, v_bf = _proj(x_bf, w_in, j, QC + KC, KC, None, 0, [F32, BF16], tm, 512)
    (qi_bf,) = _proj(x_bf, w_in, j, QC + 2 * KC, QI, grp.tab_i, IDX_ROT_DIM // 2, [BF16], tm, 512)
    c, s_hi, s_lo = grp.tab_i
    lane = jnp.arange(LANES)[None, :]
    keep = lane < IDX_DIM
    tab_kw = (jnp.where(keep, c, 1.0), jnp.where(keep, s_hi, 0.0), jnp.where(keep, s_lo, 0.0))
    n_tail = w_in.shape[2] - (QC + 2 * KC + QI)
    w_tail = jnp.pad(w_in[j:j + 1, :, QC + 2 * KC + QI:], ((0, 0), (0, 0), (0, LANES - n_tail)))
    (kiwi,) = _proj(x_bf, w_tail, 0, 0, LANES, tab_kw, IDX_ROT_DIM // 2, [F32], tm, LANES)
    return q_bf, k_f, k_bf, v_f, v_bf, qi_bf, kiwi


def _ffn(grp, x, x_bf, i, hist, w_gate, w_up, w_c, b_c, w_down, ln_g, ln_b, alpha, tf):
    F = w_gate.shape[2]
    H = hist.shape[0] // (grp.rows // grp.seq_rows)
    h_bf, g_tail = _ffn_up(x_bf, w_gate, w_up, i, hist, w_c[i], b_c[i],
                           tm=grp.seq_rows, tf=tf, stride=grp.stride, hist_rows=H)
    x, x_bf = _mm_res_ln(h_bf, w_down, i, jnp.zeros((w_down.shape[2],), F32), x, ln_g[i], ln_b[i],
                         alpha, grp.tm_down, 256)
    return x, x_bf, g_tail


def kernel(x_prompt, x_sample, cache_k, cache_v, cache_kidx, state_conv, state_ffn, page_table,
           w_attn_in, w_attn_out, w_pw1, b_pw1, w_dw, b_dw, ln_conv_g, ln_conv_b, w_pw2, b_pw2,
           w_ffn_gate, w_ffn_up, w_ffn_conv, b_ffn_conv, w_ffn_down,
           ln_mix_g, ln_mix_b, ln_ffn_g, ln_ffn_b):
    B, T, D = x_prompt.shape
    Bd, Tn, _ = x_sample.shape
    depth = w_ffn_gate.shape[0]
    F = w_ffn_gate.shape[2]
    n_attn, n_pool, page = cache_k.shape[:3]
    n_conv = w_pw1.shape[0]
    cw = w_dw.shape[1]
    past = page_table.shape[1] * page
    KC = N_KV_HEADS * HEAD_DIM
    alpha = float((2 * depth) ** 0.25)
    assert Bd == SUBLANES and Tn <= T_PAD and D == N_HEADS * HEAD_DIM
    tf = 256 if F % 256 == 0 else LANES

    gp = _Group(B * T, T, 1, jnp.arange(T, dtype=I32), min(1024, T), min(512, T))
    pos_s = past + jnp.arange(Tn, dtype=I32)
    gs = _Group(Tn * Bd, Tn * Bd, Bd, jnp.repeat(pos_s, Bd), Tn * Bd, Tn * Bd)

    xp = x_prompt.reshape(B * T, D)
    xs = x_sample.transpose(1, 0, 2).reshape(Tn * Bd, D)
    xp_bf, xs_bf = xp.astype(BF16), xs.astype(BF16)

    ck = cache_k.reshape(n_attn * n_pool, page, KC)
    cv = cache_v.reshape(n_attn * n_pool, page, KC)
    cki = cache_kidx.reshape(n_attn * n_pool, page, IDX_DIM)
    conv_hist_s = state_conv.transpose(0, 2, 1, 3).reshape(n_conv, (cw - 1) * Bd, D)
    ffn_hist_s = state_ffn.transpose(0, 2, 1, 3).reshape(depth, 2 * Bd, F)
    ffn_hist_p = jnp.zeros((B * SUBLANES, F), F32)
    zero_d = jnp.zeros((D,), F32)

    k_sel_p = min(TOPK_MAX, T // 4)
    k_sel_s = min(TOPK_MAX, (past + Tn) // 4)

    outs = {name: [] for name in ("kp", "vp", "kip", "convp", "ffnp", "ks", "vs", "kis", "convs", "ffns")}

    for i in range(depth):
        j = i // 2
        if i % 2 == 0:
            q_bf, k_f, k_bf, v_f, v_bf, qi_bf, kiwi = _in_proj(gp, xp_bf, w_attn_in, j)
            wit = kiwi[:, IDX_DIM:IDX_DIM + N_IDX_HEADS].T
            vt_bf = v_bf.reshape(B, T, KC).transpose(0, 2, 1)
            o_bf = _attn_prompt(qi_bf, wit, kiwi, q_bf, k_bf, vt_bf, B, T, k_sel_p)
            xp, xp_bf = _mm_res_ln(o_bf, w_attn_out, j, zero_d, xp, ln_mix_g[i], ln_mix_b[i],
                                   alpha, gp.tm_down, 512)
            outs["kp"].append(k_f.reshape(B, T, N_KV_HEADS, HEAD_DIM))
            outs["vp"].append(v_f.reshape(B, T, N_KV_HEADS, HEAD_DIM))
            outs["kip"].append(kiwi[:, :IDX_DIM].reshape(B, T, IDX_DIM))

            q_bf, k_f, k_bf, v_f, v_bf, qi_bf, kiwi = _in_proj(gs, xs_bf, w_attn_in, j)
            tb = lambda a: a.reshape(Tn, Bd, -1).transpose(1, 0, 2)
            pad_rows = lambda a: jnp.pad(a, ((0, 0), (0, LANES - Tn), (0, 0)))
            kinew = pad_rows(tb(kiwi[:, :IDX_DIM]))
            knew, vnew = pad_rows(tb(k_f)), pad_rows(tb(v_f))
            qi4 = tb(qi_bf).reshape(Bd, Tn, N_IDX_HEADS, IDX_DIM)
            qi4 = jnp.pad(qi4, ((0, 0), (0, T_PAD - Tn), (0, 0), (0, 0)))
            qit = qi4.transpose(0, 3, 2, 1).reshape(Bd, IDX_DIM, N_IDX_HEADS * T_PAD)
            wi4 = tb(kiwi[:, IDX_DIM:IDX_DIM + N_IDX_HEADS]) * IDX_W_SCALE
            wi4 = jnp.pad(wi4, ((0, 0), (0, T_PAD - Tn), (0, 0)))
            w_lane = wi4.transpose(0, 2, 1).reshape(Bd, 1, N_IDX_HEADS * T_PAD)
            score_tiled = _samp_score(page_table, cki, j * n_pool, kinew, qit, w_lane, Tn)
            s_tot = score_tiled.shape[1]
            score_t = score_tiled[:, :, :T_PAD].transpose(1, 0, 2).reshape(s_tot, Bd * T_PAD)
            score_t = jnp.pad(score_t, ((0, 0), (0, LANES - Bd * T_PAD)))
            thr, cut = _samp_select(score_t, k_sel_s, Bd * T_PAD, Tn)
            per_seq = lambda a: jnp.tile(a[0, :Bd * T_PAD].reshape(Bd, 1, T_PAD), (1, 1, LANES // T_PAD))
            q4 = tb(q_bf).reshape(Bd, Tn, N_KV_HEADS, GROUP, HEAD_DIM)
            q4 = jnp.pad(q4, ((0, 0), (0, T_PAD - Tn), (0, 0), (0, 0), (0, 0)))
            qcols = q4.transpose(0, 2, 4, 3, 1).reshape(Bd, N_KV_HEADS, HEAD_DIM, GROUP * T_PAD)
            eye = jnp.eye(N_KV_HEADS, dtype=BF16)
            qbd = (qcols[:, :, :, None, :] * eye[None, :, None, :, None]).reshape(Bd, KC, LANES)
            ot = _samp_attend(page_table, ck, cv, j * n_pool, knew, vnew, score_tiled,
                              per_seq(thr), per_seq(cut), qbd)
            ot = ot.reshape(Bd, N_KV_HEADS, HEAD_DIM, N_KV_HEADS, GROUP, T_PAD)
            nn = jnp.arange(N_KV_HEADS)
            o = ot[:, nn, :, nn]
            o = o[..., :Tn].transpose(4, 1, 0, 3, 2).reshape(Tn * Bd, N_HEADS * HEAD_DIM)
            xs, xs_bf = _mm_res_ln(o.astype(BF16), w_attn_out, j, zero_d, xs, ln_mix_g[i], ln_mix_b[i],
                                   alpha, gs.tm_down, 512)
            outs["ks"].append(tb(k_f).reshape(Bd, Tn, N_KV_HEADS, HEAD_DIM))
            outs["vs"].append(tb(v_f).reshape(Bd, Tn, N_KV_HEADS, HEAD_DIM))
            outs["kis"].append(tb(kiwi[:, :IDX_DIM]))
        else:
            a = _glu(xp_bf, w_pw1, b_pw1, j, gp.tm, 512)
            h_bf = _dwconv(a, None, w_dw[j], b_dw[j], ln_conv_g[j], ln_conv_b[j],
                           tm=min(256, T), stride=1, hist_rows=32, seq_tiles=T // min(256, T))
            xp, xp_bf = _mm_res_ln(h_bf, w_pw2, j, b_pw2[j], xp, ln_mix_g[i], ln_mix_b[i],
                                   alpha, gp.tm_down, 512)
            outs["convp"].append(a.reshape(B, T, D)[:, T - (cw - 1):])

            a = _glu(xs_bf, w_pw1, b_pw1, j, gs.tm, 512)
            h_bf = _dwconv(a, conv_hist_s[j], w_dw[j], b_dw[j], ln_conv_g[j], ln_conv_b[j],
                           tm=Tn * Bd, stride=Bd, hist_rows=(cw - 1) * Bd, seq_tiles=0)
            xs, xs_bf = _mm_res_ln(h_bf, w_pw2, j, b_pw2[j], xs, ln_mix_g[i], ln_mix_b[i],
                                   alpha, gs.tm_down, 512)
            a_pad = jnp.concatenate([conv_hist_s[j], a], axis=0)[Tn * Bd:]
            outs["convs"].append(a_pad.reshape(cw - 1, Bd, D).transpose(1, 0, 2))

        xp, xp_bf, g_tail = _ffn(gp, xp, xp_bf, i, ffn_hist_p, w_ffn_gate, w_ffn_up, w_ffn_conv,
                                 b_ffn_conv, w_ffn_down, ln_ffn_g, ln_ffn_b, alpha, tf)
        outs["ffnp"].append(g_tail.reshape(B, SUBLANES, F)[:, SUBLANES - 2:])
        xs, xs_bf, g_tail = _ffn(gs, xs, xs_bf, i, ffn_hist_s[i], w_ffn_gate, w_ffn_up, w_ffn_conv,
                                 b_ffn_conv, w_ffn_down, ln_ffn_g, ln_ffn_b, alpha, tf)
        outs["ffns"].append(g_tail.reshape(2, Bd, F).transpose(1, 0, 2))

    st = lambda name: jnp.stack(outs[name])
    y_prompt = xp.reshape(B, T, D)
    y_sample = xs.reshape(Tn, Bd, D).transpose(1, 0, 2)
    return (y_prompt, y_sample, st("kp"), st("vp"), st("kip"), st("convp"), st("ffnp"),
            st("ks"), st("vs"), st("kis"), st("convs"), st("ffns"))
```

```python
import functools

import jax
import jax.numpy as jnp
from jax import lax
from jax.experimental import pallas as pl
from jax.experimental.pallas import tpu as pltpu

F32 = jnp.float32
BF16 = jnp.bfloat16
I32 = jnp.int32

N_HEADS = 16
N_KV_HEADS = 4
HEAD_DIM = 128
GROUP = N_HEADS // N_KV_HEADS
ROT_DIM = HEAD_DIM // 4
ROPE_THETA = 500000.0
N_IDX_HEADS = 16
IDX_DIM = 64
IDX_ROT_DIM = IDX_DIM // 4
IDX_W_SCALE = (N_IDX_HEADS * IDX_DIM) ** -0.5
TOPK_MAX = 256
Q_BLOCK = 128
LN_EPS = 1e-5
LANES = 128
SUBLANES = 8
PAGES_PER_STEP = 8
T_PAD = 8
NEG_BIG = -1e30
INT_MIN = -(2 ** 31)
INT_MAX = 2 ** 31 - 1
NEG_INF_KEY = INT_MIN + 0x007FFFFF
VMEM_LIMIT = 56 * 2 ** 20


def _params(*sem):
    return pltpu.CompilerParams(dimension_semantics=sem, vmem_limit_bytes=VMEM_LIMIT)


def _dot(a, b):
    return jnp.dot(a, b, preferred_element_type=F32)


def _dot_nt(a, b):
    return lax.dot_general(a, b, (((1,), (1,)), ((), ())), preferred_element_type=F32)


def _layer_norm(z, g, b):
    mu = jnp.mean(z, axis=-1, keepdims=True)
    zc = z - mu
    var = jnp.mean(zc * zc, axis=-1, keepdims=True)
    return zc * lax.rsqrt(var + LN_EPS) * g + b


def _lane_tile(x, rep):
    return x if rep == 1 else jnp.concatenate([x] * rep, axis=1)


def _proj_body(*refs, shift):
    x_ref, w_ref = refs[0], refs[1]
    if shift:
        c_ref, s1_ref, s2_ref = refs[2:5]
        outs = refs[5:]
    else:
        outs = refs[2:]
    y = _dot(x_ref[...], w_ref[...].astype(BF16))
    if shift:
        tn = y.shape[1]
        rep = tn // LANES
        y = (y * _lane_tile(c_ref[...], rep)
             + pltpu.roll(y, shift, 1) * _lane_tile(s1_ref[...], rep)
             + pltpu.roll(y, tn - shift, 1) * _lane_tile(s2_ref[...], rep))
    for o in outs:
        o[...] = y.astype(o.dtype)


def _proj(x_bf, w, layer, col_off, ncols, tabs, shift, out_dtypes, tm, tn):
    M, K = x_bf.shape
    assert col_off % tn == 0 and ncols % tn == 0 and M % tm == 0
    cb = col_off // tn
    in_specs = [pl.BlockSpec((tm, K), lambda i, j: (i, 0)),
                pl.BlockSpec((None, K, tn), lambda i, j: (layer, 0, cb + j))]
    args = [x_bf, w]
    if shift:
        period = tabs[0].shape[0] // tm
        for t in tabs:
            in_specs.append(pl.BlockSpec((tm, LANES), lambda i, j: (i % period, 0)))
            args.append(t)
    outs = pl.pallas_call(
        functools.partial(_proj_body, shift=shift),
        grid=(M // tm, ncols // tn),
        in_specs=in_specs,
        out_specs=[pl.BlockSpec((tm, tn), lambda i, j: (i, j)) for _ in out_dtypes],
        out_shape=[jax.ShapeDtypeStruct((M, ncols), dt) for dt in out_dtypes],
        compiler_params=_params("parallel", "arbitrary"),
        name="proj",
    )(*args)
    return outs


def _glu_body(x_ref, wa_ref, wg_ref, ba_ref, bg_ref, a_ref):
    x = x_ref[...]
    a = _dot(x, wa_ref[...].astype(BF16)) + ba_ref[...]
    g = _dot(x, wg_ref[...].astype(BF16)) + bg_ref[...]
    a_ref[...] = a * jax.nn.sigmoid(g)


def _glu(x_bf, w_pw1, b_pw1, layer, tm, tn):
    M, K = x_bf.shape
    C = w_pw1.shape[2] // 2
    nb = C // tn
    b2 = b_pw1[layer].reshape(1, 2 * C)
    return pl.pallas_call(
        _glu_body,
        grid=(M // tm, nb),
        in_specs=[pl.BlockSpec((tm, K), lambda i, j: (i, 0)),
                  pl.BlockSpec((None, K, tn), lambda i, j: (layer, 0, j)),
                  pl.BlockSpec((None, K, tn), lambda i, j: (layer, 0, nb + j)),
                  pl.BlockSpec((1, tn), lambda i, j: (0, j)),
                  pl.BlockSpec((1, tn), lambda i, j: (0, nb + j))],
        out_specs=pl.BlockSpec((tm, tn), lambda i, j: (i, j)),
        out_shape=jax.ShapeDtypeStruct((M, C), F32),
        compiler_params=_params("parallel", "arbitrary"),
        name="glu",
    )(x_bf, w_pw1, w_pw1, b2, b2)


def _dwconv_body(a_ref, hist_ref, w_ref, b_ref, g_ref, beta_ref, h_ref, ext_ref, conv_ref, sh_ref,
                 *, stride, hist_rows, width, seq_tiles, row_block):
    tm, C = a_ref.shape
    R = hist_rows + tm
    hist = hist_ref[...]
    if seq_tiles:
        hist = jnp.where(pl.program_id(0) % seq_tiles == 0, 0.0, hist)
    ext_ref[0:hist_rows, :] = hist
    ext_ref[hist_rows:R, :] = a_ref[...]
    tap_off = [hist_rows - (width - 1 - k) * stride for k in range(width)]
    residues = sorted({off % SUBLANES for off in tap_off} - {0})

    def chunk(ci, carry):
        c0 = pl.multiple_of(ci * LANES, LANES)
        wch = w_ref[:, pl.ds(c0, LANES)]
        bch = b_ref[:, pl.ds(c0, LANES)]
        for r in residues:
            sh_ref[r, 0:R - SUBLANES, :] = ext_ref[pl.ds(r, R - SUBLANES), pl.ds(c0, LANES)]
        for r0 in range(0, tm, row_block):
            acc = jnp.broadcast_to(bch, (row_block, LANES))
            for k in range(width):
                r = tap_off[k] % SUBLANES
                base = tap_off[k] - r + r0
                if r == 0:
                    tap = ext_ref[pl.ds(base, row_block), pl.ds(c0, LANES)]
                else:
                    tap = sh_ref[r, pl.ds(base, row_block), :]
                acc = acc + tap * wch[k:k + 1, :]
            conv_ref[pl.ds(r0, row_block), pl.ds(c0, LANES)] = acc
        return carry

    lax.fori_loop(0, C // LANES, chunk, 0)
    y = _layer_norm(conv_ref[...], g_ref[...], beta_ref[...])
    h_ref[...] = (y * jax.nn.sigmoid(y)).astype(h_ref.dtype)


def _dwconv(a, hist, w_dw, b_dw, ln_g, ln_b, *, tm, stride, hist_rows, seq_tiles):
    M, C = a.shape
    width = w_dw.shape[0]
    if seq_tiles:
        per = tm // hist_rows
        hist_arr = a
        hist_spec = pl.BlockSpec((hist_rows, C), lambda i: (jnp.maximum(i * per - 1, 0), 0))
    else:
        hist_arr = hist
        hist_spec = pl.BlockSpec((hist_rows, C), lambda i: (0, 0))
    vec = lambda: pl.BlockSpec((1, C), lambda i: (0, 0))
    return pl.pallas_call(
        functools.partial(_dwconv_body, stride=stride, hist_rows=hist_rows, width=width,
                          seq_tiles=seq_tiles, row_block=min(tm, 64)),
        grid=(M // tm,),
        in_specs=[pl.BlockSpec((tm, C), lambda i: (i, 0)), hist_spec,
                  pl.BlockSpec((width, C), lambda i: (0, 0)), vec(), vec(), vec()],
        out_specs=pl.BlockSpec((tm, C), lambda i: (i, 0)),
        out_shape=jax.ShapeDtypeStruct((M, C), BF16),
        scratch_shapes=[pltpu.VMEM((hist_rows + tm, C), F32), pltpu.VMEM((tm, C), F32),
                        pltpu.VMEM((SUBLANES, hist_rows + tm, LANES), F32)],
        compiler_params=_params("arbitrary"),
        name="dwconv",
    )(a, hist_arr, w_dw, b_dw.reshape(1, C), ln_g.reshape(1, C), ln_b.reshape(1, C))


def _mm_res_ln_body(a_ref, w_ref, bias_ref, res_ref, g_ref, b_ref, xo_ref, xb_ref, *, nk, alpha):
    k = pl.program_id(1)
    a = a_ref[...]
    N = xo_ref.shape[1]
    nc = min(N, 512)
    for c0 in range(0, N, nc):
        y = _dot(a, w_ref[:, c0:c0 + nc].astype(BF16))

        @pl.when(k == 0)
        def _():
            xo_ref[:, c0:c0 + nc] = y

        @pl.when(k > 0)
        def _():
            xo_ref[:, c0:c0 + nc] += y

    @pl.when(k == nk - 1)
    def _():
        z = alpha * res_ref[...] + (xo_ref[...] + bias_ref[...])
        out = _layer_norm(z, g_ref[...], b_ref[...])
        xo_ref[...] = out
        xb_ref[...] = out.astype(BF16)


def _mm_res_ln(a_bf, w, layer, bias, res, ln_g, ln_b, alpha, tm, tk):
    M, K = a_bf.shape
    N = w.shape[2]
    nk = K // tk
    vec = lambda: pl.BlockSpec((1, N), lambda i, k: (0, 0))
    return pl.pallas_call(
        functools.partial(_mm_res_ln_body, nk=nk, alpha=alpha),
        grid=(M // tm, nk),
        in_specs=[pl.BlockSpec((tm, tk), lambda i, k: (i, k)),
                  pl.BlockSpec((None, tk, N), lambda i, k: (layer, k, 0)),
                  vec(),
                  pl.BlockSpec((tm, N), lambda i, k: (i, 0), pipeline_mode=pl.Buffered(1)),
                  vec(), vec()],
        out_specs=[pl.BlockSpec((tm, N), lambda i, j: (i, 0)),
                   pl.BlockSpec((tm, N), lambda i, j: (i, 0))],
        out_shape=[jax.ShapeDtypeStruct((M, N), F32), jax.ShapeDtypeStruct((M, N), BF16)],
        compiler_params=_params("parallel", "arbitrary"),
        name="mm_res_ln",
    )(a_bf, w, bias.reshape(1, N), res, ln_g.reshape(1, N), ln_b.reshape(1, N))


def _ffn_up_body(x_ref, wg_ref, wu_ref, hist_ref, wc_ref, bc_ref, h_ref, gt_ref, ext_ref,
                 *, stride, hist_rows):
    x = x_ref[...]
    tm = x.shape[0]
    H = hist_rows
    g = _dot(x, wg_ref[...].astype(BF16))
    ext_ref[0:H, :] = hist_ref[...]
    ext_ref[H:H + tm, :] = g
    gt_ref[...] = g[tm - H:tm, :]
    u = _dot(x, wu_ref[...].astype(BF16))
    wc = wc_ref[...]
    c = (bc_ref[...] + wc[2:3, :] * g
         + wc[1:2, :] * ext_ref[H - stride:H - stride + tm, :]
         + wc[0:1, :] * ext_ref[H - 2 * stride:H - 2 * stride + tm, :])
    h_ref[...] = (c * jax.nn.sigmoid(c) * u).astype(h_ref.dtype)


def _ffn_up(x_bf, w_gate, w_up, layer, hist, w_c, b_c, *, tm, tf, stride, hist_rows):
    M, K = x_bf.shape
    F = w_gate.shape[2]
    H = hist_rows
    return pl.pallas_call(
        functools.partial(_ffn_up_body, stride=stride, hist_rows=H),
        grid=(M // tm, F // tf),
        in_specs=[pl.BlockSpec((tm, K), lambda i, j: (i, 0)),
                  pl.BlockSpec((None, K, tf), lambda i, j: (layer, 0, j)),
                  pl.BlockSpec((None, K, tf), lambda i, j: (layer, 0, j)),
                  pl.BlockSpec((H, tf), lambda i, j: (i, j)),
                  pl.BlockSpec((3, tf), lambda i, j: (0, j)),
                  pl.BlockSpec((1, tf), lambda i, j: (0, j))],
        out_specs=[pl.BlockSpec((tm, tf), lambda i, j: (i, j)),
                   pl.BlockSpec((H, tf), lambda i, j: (i, j))],
        out_shape=[jax.ShapeDtypeStruct((M, F), BF16),
                   jax.ShapeDtypeStruct((M // tm * H, F), F32)],
        scratch_shapes=[pltpu.VMEM((H + tm, tf), F32)],
        compiler_params=_params("parallel", "arbitrary"),
        name="ffn_up",
    )(x_bf, w_gate, w_up, hist, w_c, b_c.reshape(1, F))


def _order_key(score):
    bits = pltpu.bitcast(score + 0.0, I32)
    return jnp.where(bits < 0, bits ^ INT_MAX, bits)


COUNT_CHAINS = 8


def _count(mask):
    S, L = mask.shape
    assert S < 2 ** 24
    x = jnp.where(mask, 1.0, 0.0).astype(F32)
    group = COUNT_CHAINS * SUBLANES
    if S % group == 0 and S > group:
        chains = [x[c * SUBLANES:(c + 1) * SUBLANES] for c in range(COUNT_CHAINS)]
        for r in range(group, S, group):
            for c in range(COUNT_CHAINS):
                chains[c] = chains[c] + x[r + c * SUBLANES:r + (c + 1) * SUBLANES]
        while len(chains) > 1:
            chains = [chains[i] + chains[i + 1] for i in range(0, len(chains), 2)]
        x = chains[0]
    return jnp.sum(x, axis=0, keepdims=True).astype(I32)


def _select_topk(key_ref, cut_ref, k, lane_ok):
    S, L = key_ref.shape
    cnt0 = _count(key_ref[...] >= 0)
    base = jnp.where(cnt0 >= k, 0, INT_MIN).astype(I32)

    def bit_step(t, base):
        cand = base | lax.shift_left(jnp.int32(1), 30 - t)
        return jnp.where(_count(key_ref[...] >= cand) >= k, cand, base)

    thr = lax.fori_loop(0, 31, bit_step, base)
    key = key_ref[...]
    n_gt = _count(key > thr)
    n_ge = _count(key >= thr)
    need = k - n_gt
    tie = (n_ge > k) & (thr > NEG_INF_KEY) & lane_ok
    cut_ref[...] = jnp.full((1, L), INT_MAX, I32)
    n_tie = jnp.sum(jnp.where(tie, 1, 0).astype(I32))

    @pl.when(n_tie > 0)
    def _():
        nbits = max(1, (S - 1).bit_length())
        rows = lax.broadcasted_iota(I32, (S, L), 0)

        def idx_step(t, pos):
            cand = pos | lax.shift_left(jnp.int32(1), nbits - 1 - t)
            c = _count((key_ref[...] == thr) & (rows < cand))
            return jnp.where(c < need, cand, pos)

        cut_ref[...] = lax.fori_loop(0, nbits, idx_step, jnp.zeros((1, L), I32))

    return thr


def _attn_prompt_body(qi_ref, wit_ref, kiwi_ref, q_ref, k_ref, v_ref, o_ref,
                      ki2_ref, sc_ref, key_ref, cut_ref, vt_ref, *, k_sel, n_var):
    S = kiwi_ref.shape[0]
    tq = qi_ref.shape[0]
    qb = pl.program_id(1)

    @pl.when(qb == 0)
    def _():
        lane = lax.broadcasted_iota(I32, (S, LANES), 1)
        lo = jnp.where(lane < IDX_DIM, kiwi_ref[...], 0.0)
        ki2_ref[0:S, :] = lo.astype(BF16)
        ki2_ref[S:2 * S, :] = pltpu.roll(lo, IDX_DIM, 1).astype(BF16)
        for n in range(N_KV_HEADS):
            rows = slice(n * HEAD_DIM, (n + 1) * HEAD_DIM)
            vt_ref[rows, :] = v_ref[:, rows].T.astype(BF16)

    ws = wit_ref[...] * IDX_W_SCALE
    scale = HEAD_DIM ** -0.5

    def attend(su):
        sc = sc_ref.at[0:su]
        keys = key_ref.at[0:su]
        for p in range(N_IDX_HEADS // 2):
            qp = qi_ref[:, p * LANES:(p + 1) * LANES]
            d0 = _dot_nt(ki2_ref[0:su, :], qp)
            d1 = _dot_nt(ki2_ref[S:S + su, :], qp)
            part = (ws[2 * p:2 * p + 1, :] * jnp.maximum(d0, 0.0)
                    + ws[2 * p + 1:2 * p + 2, :] * jnp.maximum(d1, 0.0))
            if p == 0:
                sc[...] = part
            else:
                sc[...] += part

        rows = lax.broadcasted_iota(I32, (su, tq), 0)
        qpos = qb * tq + lax.broadcasted_iota(I32, (su, tq), 1)
        causal = rows <= qpos
        keys[...] = _order_key(jnp.where(causal, sc[...], -jnp.inf))
        thr = _select_topk(keys, cut_ref, k_sel, jnp.full((1, tq), True))
        key = keys[...]
        sel = ((key > thr) | ((key == thr) & (rows <= cut_ref[...]))) & causal
        bias = _lane_tile(jnp.where(sel, 0.0, NEG_BIG), GROUP)

        for n in range(N_KV_HEADS):
            qs = jnp.concatenate([q_ref[:, (n * GROUP + g) * HEAD_DIM:(n * GROUP + g + 1) * HEAD_DIM]
                                  for g in range(GROUP)], axis=0)
            s = _dot_nt(k_ref[0:su, n * HEAD_DIM:(n + 1) * HEAD_DIM], qs) * scale + bias
            m = jnp.max(s, axis=0, keepdims=True)
            e = jnp.exp(s - m)
            l = jnp.sum(e, axis=0, keepdims=True)
            pt = (e * (1.0 / l)).astype(BF16)
            ot = _dot(vt_ref[n * HEAD_DIM:(n + 1) * HEAD_DIM, 0:su], pt)
            for g in range(GROUP):
                h = n * GROUP + g
                o_ref[:, h * HEAD_DIM:(h + 1) * HEAD_DIM] = (
                    ot[:, g * tq:(g + 1) * tq].T.astype(o_ref.dtype))

    nb = S // tq
    per = nb // n_var
    for c in range(n_var):
        pl.when(qb // per == c)(functools.partial(attend, (c + 1) * per * tq))


def _attn_prompt(qi_bf, wit, kiwi, q_bf, k_bf, v_f, B, T, k_sel):
    tq = Q_BLOCK
    nb = T // tq
    n_var = 4 if nb % 4 == 0 else 1
    QC = N_HEADS * HEAD_DIM
    KC = N_KV_HEADS * HEAD_DIM
    QI = N_IDX_HEADS * IDX_DIM
    return pl.pallas_call(
        functools.partial(_attn_prompt_body, k_sel=k_sel, n_var=n_var),
        grid=(B, nb),
        in_specs=[pl.BlockSpec((tq, QI), lambda b, i: (b * nb + i, 0)),
                  pl.BlockSpec((N_IDX_HEADS, tq), lambda b, i: (0, b * nb + i)),
                  pl.BlockSpec((T, LANES), lambda b, i: (b, 0)),
                  pl.BlockSpec((tq, QC), lambda b, i: (b * nb + i, 0)),
                  pl.BlockSpec((T, KC), lambda b, i: (b, 0)),
                  pl.BlockSpec((T, KC), lambda b, i: (b, 0))],
        out_specs=pl.BlockSpec((tq, QC), lambda b, i: (b * nb + i, 0)),
        out_shape=jax.ShapeDtypeStruct((B * T, QC), BF16),
        scratch_shapes=[pltpu.VMEM((2 * T, LANES), BF16), pltpu.VMEM((T, tq), F32),
                        pltpu.VMEM((T, tq), I32), pltpu.VMEM((1, tq), I32),
                        pltpu.VMEM((KC, T), BF16)],
        compiler_params=_params("parallel", "arbitrary"),
        name="attn_prompt",
    )(qi_bf, wit, kiwi, q_bf, k_bf, v_f)


def _page_scores(kit, qi, w_col):
    d = _dot(qi, kit.astype(BF16))
    z = jnp.maximum(d, 0.0) * w_col
    return z.reshape(N_IDX_HEADS, T_PAD, LANES).sum(axis=0)


def _samp_score_body(pt_ref, *refs, n_tok):
    G = PAGES_PER_STEP
    pages = refs[:G]
    kinew_ref, qi_ref, wc_ref, out_ref = refs[G:G + 4]
    p = pl.program_id(1)
    n_steps = pl.num_programs(1)
    qi = qi_ref[...]
    wc = wc_ref[...]

    @pl.when(p < n_steps - 1)
    def _():
        for g in range(G):
            out_ref[:, g * LANES:(g + 1) * LANES] = _page_scores(pages[g][...], qi, wc)

    @pl.when(p == n_steps - 1)
    def _():
        z = _page_scores(kinew_ref[...], qi, wc)
        tok = lax.broadcasted_iota(I32, (T_PAD, LANES), 0)
        new = lax.broadcasted_iota(I32, (T_PAD, LANES), 1)
        ok = (new <= tok) & (new < n_tok)
        out_ref[:, 0:LANES] = jnp.where(ok, z, -jnp.inf)
        out_ref[:, LANES:G * LANES] = jnp.full((T_PAD, (G - 1) * LANES), -jnp.inf, F32)


def _page_index_map(g, layer_off, n_steps, rank=3):
    def index_map(b, p, pt):
        pg = jnp.minimum(p, n_steps - 2) * PAGES_PER_STEP + g % PAGES_PER_STEP
        return (layer_off + pt[b, pg],) + (0,) * (rank - 1)
    return index_map


def _samp_score(page_table, cache_kit, layer_off, kitnew, qi_rows, w_col, n_tok):
    Bd, n_pages = page_table.shape
    G = PAGES_PER_STEP
    n_steps = n_pages // G + 1
    page = cache_kit.shape[2]
    in_specs = [pl.BlockSpec((None, IDX_DIM, page), _page_index_map(g, layer_off, n_steps))
                for g in range(G)]
    in_specs += [pl.BlockSpec((None, IDX_DIM, LANES), lambda b, p, pt: (b, 0, 0)),
                 pl.BlockSpec((None, LANES, IDX_DIM), lambda b, p, pt: (b, 0, 0)),
                 pl.BlockSpec((None, LANES, 1), lambda b, p, pt: (b, 0, 0))]
    return pl.pallas_call(
        functools.partial(_samp_score_body, n_tok=n_tok),
        grid_spec=pltpu.PrefetchScalarGridSpec(
            num_scalar_prefetch=1, grid=(Bd, n_steps), in_specs=in_specs,
            out_specs=pl.BlockSpec((None, T_PAD, G * page), lambda b, p, pt: (b, 0, p))),
        out_shape=jax.ShapeDtypeStruct((Bd, T_PAD, n_steps * G * page), F32),
        compiler_params=_params("parallel", "arbitrary"),
        name="samp_score",
    )(page_table, *([cache_kit] * G), kitnew, qi_rows, w_col)


def _samp_select_body(sc_ref, thr_ref, cut_ref, key_ref, *, k_sel, n_lanes, n_tok):
    key_ref[...] = _order_key(sc_ref[...])
    lane = lax.broadcasted_iota(I32, (1, LANES), 1)
    lane_ok = (lane < n_lanes) & (lane % T_PAD < n_tok)
    thr_ref[...] = _select_topk(key_ref, cut_ref, k_sel, lane_ok)


def _samp_select(score_t, k_sel, n_lanes, n_tok):
    S = score_t.shape[0]
    return pl.pallas_call(
        functools.partial(_samp_select_body, k_sel=k_sel, n_lanes=n_lanes, n_tok=n_tok),
        out_shape=[jax.ShapeDtypeStruct((1, LANES), I32), jax.ShapeDtypeStruct((1, LANES), I32)],
        scratch_shapes=[pltpu.VMEM((S, LANES), I32)],
        compiler_params=pltpu.CompilerParams(vmem_limit_bytes=VMEM_LIMIT),
        name="samp_select",
    )(score_t)


def _samp_attend_body(pt_ref, *refs):
    G = PAGES_PER_STEP
    kp = refs[:G]
    vp = refs[G:2 * G]
    (knew_ref, vnew_ref, sc_ref, thr_ref, cut_ref, qcat_ref, o_ref,
     m_ref, l_ref, acc_ref) = refs[2 * G:]
    p = pl.program_id(1)
    n_steps = pl.num_programs(1)
    scale = HEAD_DIM ** -0.5
    R = knew_ref.shape[0]

    @pl.when(p == 0)
    def _():
        m_ref[...] = jnp.full(m_ref.shape, NEG_BIG, F32)
        l_ref[...] = jnp.zeros(l_ref.shape, F32)
        acc_ref[...] = jnp.zeros(acc_ref.shape, F32)

    thr = thr_ref[...]
    cut = cut_ref[...]
    qcat = qcat_ref[...]
    rr = lax.broadcasted_iota(I32, (R, LANES), 0)
    cc = lax.broadcasted_iota(I32, (R, LANES), 1)
    head_bias = jnp.where(rr % N_KV_HEADS == cc // (GROUP * T_PAD), 0.0, NEG_BIG)
    rep = jnp.where(rr // N_KV_HEADS == cc, 1.0, 0.0).astype(BF16)

    def update(k_ref, v_ref, g):
        s = _dot(k_ref[...].astype(BF16), qcat) * scale
        key = _order_key(sc_ref[g * LANES:(g + 1) * LANES, :])
        row = (p * G + g) * LANES + lax.broadcasted_iota(I32, (LANES, LANES), 0)
        sel = (key > thr) | ((key == thr) & (row <= cut))
        bias = jnp.where(sel, 0.0, NEG_BIG).astype(BF16)
        s = s + _dot(rep, bias) + head_bias
        m_old = m_ref[...]
        m_new = jnp.maximum(m_old, jnp.max(s, axis=0, keepdims=True))
        corr = jnp.exp(m_old - m_new)
        e = jnp.exp(s - m_new)
        l_ref[...] = l_ref[...] * corr + jnp.sum(e, axis=0, keepdims=True)
        m_ref[...] = m_new
        acc_ref[...] = acc_ref[...] * corr + _dot(v_ref[...].T.astype(BF16), e.astype(BF16))

    @pl.when(p < n_steps - 1)
    def _():
        for g in range(G):
            update(kp[g], vp[g], g)

    @pl.when(p == n_steps - 1)
    def _():
        update(knew_ref, vnew_ref, 0)
        o_ref[...] = acc_ref[...] * (1.0 / l_ref[...])


def _samp_attend(page_table, cache_k, cache_v, layer_off, knew, vnew, score_tiled, thr_l, cut_l, qcat):
    Bd, n_pages = page_table.shape
    G = PAGES_PER_STEP
    n_steps = n_pages // G + 1
    R = cache_k.shape[1]
    page = R // N_KV_HEADS
    in_specs = [pl.BlockSpec((None, R, HEAD_DIM), _page_index_map(g, layer_off, n_steps))
                for g in range(2 * G)]
    per_seq = lambda r, c: pl.BlockSpec((None, r, c), lambda b, p, pt: (b, 0, 0))
    in_specs += [per_seq(R, HEAD_DIM), per_seq(R, HEAD_DIM),
                 pl.BlockSpec((None, G * page, LANES), lambda b, p, pt: (b, p, 0)),
                 per_seq(1, LANES), per_seq(1, LANES), per_seq(HEAD_DIM, LANES)]
    return pl.pallas_call(
        _samp_attend_body,
        grid_spec=pltpu.PrefetchScalarGridSpec(
            num_scalar_prefetch=1, grid=(Bd, n_steps), in_specs=in_specs,
            out_specs=pl.BlockSpec((None, HEAD_DIM, LANES), lambda b, p, pt: (b, 0, 0)),
            scratch_shapes=[pltpu.VMEM((1, LANES), F32), pltpu.VMEM((1, LANES), F32),
                            pltpu.VMEM((HEAD_DIM, LANES), F32)]),
        out_shape=jax.ShapeDtypeStruct((Bd, HEAD_DIM, LANES), F32),
        compiler_params=_params("parallel", "arbitrary"),
        name="samp_attend",
    )(page_table, *([cache_k] * G), *([cache_v] * G), knew, vnew, score_tiled, thr_l, cut_l, qcat)


def _rope_tables(pos, rot_dim, period):
    half = rot_dim // 2
    inv = ROPE_THETA ** (-jnp.arange(0, rot_dim, 2, dtype=F32) / rot_dim)
    ang = pos.astype(F32)[:, None] * inv[None, :]
    cos, sin = jnp.cos(ang), jnp.sin(ang)
    lane = jnp.arange(LANES) % period
    fi = lane % half
    c = jnp.where(lane[None, :] < rot_dim, cos[:, fi], 1.0)
    s_hi = jnp.where((lane[None, :] >= half) & (lane[None, :] < rot_dim), sin[:, fi], 0.0)
    s_lo = jnp.where(lane[None, :] < half, -sin[:, fi], 0.0)
    return c, s_hi, s_lo


class _Group:
    def __init__(self, rows, seq_rows, stride, pos, tm, tm_down):
        self.rows = rows
        self.seq_rows = seq_rows
        self.stride = stride
        self.tm = tm
        self.tm_down = tm_down
        self.tab_q = _rope_tables(pos, ROT_DIM, HEAD_DIM)
        self.tab_i = _rope_tables(pos, IDX_ROT_DIM, IDX_DIM)


def _in_proj(grp, x_bf, w_in, j):
    QC = N_HEADS * HEAD_DIM
    KC = N_KV_HEADS * HEAD_DIM
    QI = N_IDX_HEADS * IDX_DIM
    tm = grp.tm
    (q_bf,) = _proj(x_bf, w_in, j, 0, QC, grp.tab_q, ROT_DIM // 2, [BF16], tm, 512)
    k_f, k_bf = _proj(x_bf, w_in, j, QC, KC, grp.tab_q, ROT_DIM // 2, [F32, BF16], tm, 512)
    (v_f,) = _proj(x_bf, w_in, j, QC + KC, KC, None, 0, [F32], tm, 512)
    (qi_bf,) = _proj(x_bf, w_in, j, QC + 2 * KC, QI, grp.tab_i, IDX_ROT_DIM // 2, [BF16], tm, 512)
    c, s_hi, s_lo = grp.tab_i
    lane = jnp.arange(LANES)[None, :]
    keep = lane < IDX_DIM
    tab_kw = (jnp.where(keep, c, 1.0), jnp.where(keep, s_hi, 0.0), jnp.where(keep, s_lo, 0.0))
    n_tail = w_in.shape[2] - (QC + 2 * KC + QI)
    w_tail = jnp.pad(w_in[j:j + 1, :, QC + 2 * KC + QI:], ((0, 0), (0, 0), (0, LANES - n_tail)))
    (kiwi,) = _proj(x_bf, w_tail, 0, 0, LANES, tab_kw, IDX_ROT_DIM // 2, [F32], tm, LANES)
    return q_bf, k_f, k_bf, v_f, qi_bf, kiwi


def _ffn(grp, x, x_bf, i, hist, w_gate, w_up, w_c, b_c, w_down, ln_g, ln_b, alpha, tf):
    F = w_gate.shape[2]
    H = hist.shape[0] // (grp.rows // grp.seq_rows)
    h_bf, g_tail = _ffn_up(x_bf, w_gate, w_up, i, hist, w_c[i], b_c[i],
                           tm=grp.seq_rows, tf=tf, stride=grp.stride, hist_rows=H)
    x, x_bf = _mm_res_ln(h_bf, w_down, i, jnp.zeros((w_down.shape[2],), F32), x, ln_g[i], ln_b[i],
                         alpha, grp.tm_down, 512)
    return x, x_bf, g_tail


def kernel(x_prompt, x_sample, cache_k, cache_v, cache_kidx, state_conv, state_ffn, page_table,
           w_attn_in, w_attn_out, w_pw1, b_pw1, w_dw, b_dw, ln_conv_g, ln_conv_b, w_pw2, b_pw2,
           w_ffn_gate, w_ffn_up, w_ffn_conv, b_ffn_conv, w_ffn_down,
           ln_mix_g, ln_mix_b, ln_ffn_g, ln_ffn_b):
    B, T, D = x_prompt.shape
    Bd, Tn, _ = x_sample.shape
    depth = w_ffn_gate.shape[0]
    F = w_ffn_gate.shape[2]
    n_attn, n_pool, page = cache_k.shape[:3]
    n_conv = w_pw1.shape[0]
    cw = w_dw.shape[1]
    past = page_table.shape[1] * page
    KC = N_KV_HEADS * HEAD_DIM
    alpha = float((2 * depth) ** 0.25)
    assert Bd == SUBLANES and Tn <= T_PAD and D == N_HEADS * HEAD_DIM
    tf = 256 if F % 256 == 0 else LANES

    gp = _Group(B * T, T, 1, jnp.arange(T, dtype=I32), min(1024, T), min(1024, T))
    pos_s = past + jnp.arange(Tn, dtype=I32)
    gs = _Group(Tn * Bd, Tn * Bd, Bd, jnp.repeat(pos_s, Bd), Tn * Bd, Tn * Bd)

    xp = x_prompt.reshape(B * T, D)
    xs = x_sample.transpose(1, 0, 2).reshape(Tn * Bd, D)
    xp_bf, xs_bf = xp.astype(BF16), xs.astype(BF16)

    ck = cache_k.reshape(n_attn * n_pool, page * N_KV_HEADS, HEAD_DIM)
    cv = cache_v.reshape(n_attn * n_pool, page * N_KV_HEADS, HEAD_DIM)
    ckit = cache_kidx.reshape(n_attn * n_pool, page, IDX_DIM).transpose(0, 2, 1)
    conv_hist_s = state_conv.transpose(0, 2, 1, 3).reshape(n_conv, (cw - 1) * Bd, D)
    ffn_hist_s = state_ffn.transpose(0, 2, 1, 3).reshape(depth, 2 * Bd, F)
    ffn_hist_p = jnp.zeros((B * SUBLANES, F), F32)
    zero_d = jnp.zeros((D,), F32)

    k_sel_p = min(TOPK_MAX, T // 4)
    k_sel_s = min(TOPK_MAX, (past + Tn) // 4)

    outs = {name: [] for name in ("kp", "vp", "kip", "convp", "ffnp", "ks", "vs", "kis", "convs", "ffns")}

    for i in range(depth):
        j = i // 2
        if i % 2 == 0:
            q_bf, k_f, k_bf, v_f, qi_bf, kiwi = _in_proj(gp, xp_bf, w_attn_in, j)
            wit = kiwi[:, IDX_DIM:IDX_DIM + N_IDX_HEADS].T
            o_bf = _attn_prompt(qi_bf, wit, kiwi, q_bf, k_bf, v_f, B, T, k_sel_p)
            xp, xp_bf = _mm_res_ln(o_bf, w_attn_out, j, zero_d, xp, ln_mix_g[i], ln_mix_b[i],
                                   alpha, gp.tm_down, 512)
            outs["kp"].append(k_f.reshape(B, T, N_KV_HEADS, HEAD_DIM))
            outs["vp"].append(v_f.reshape(B, T, N_KV_HEADS, HEAD_DIM))
            outs["kip"].append(kiwi[:, :IDX_DIM].reshape(B, T, IDX_DIM))

            q_bf, k_f, k_bf, v_f, qi_bf, kiwi = _in_proj(gs, xs_bf, w_attn_in, j)
            tb = lambda a: a.reshape(Tn, Bd, -1).transpose(1, 0, 2)
            pad_rows = lambda a: jnp.pad(a, ((0, 0), (0, LANES - Tn), (0, 0)))
            kitnew = pad_rows(tb(kiwi[:, :IDX_DIM])).transpose(0, 2, 1)
            heads = lambda a: a.reshape(Bd, LANES * N_KV_HEADS, HEAD_DIM)
            knew, vnew = heads(pad_rows(tb(k_f))), heads(pad_rows(tb(v_f)))
            qi4 = tb(qi_bf).reshape(Bd, Tn, N_IDX_HEADS, IDX_DIM)
            qi4 = jnp.pad(qi4, ((0, 0), (0, T_PAD - Tn), (0, 0), (0, 0)))
            qi_rows = qi4.transpose(0, 2, 1, 3).reshape(Bd, N_IDX_HEADS * T_PAD, IDX_DIM)
            wi4 = tb(kiwi[:, IDX_DIM:IDX_DIM + N_IDX_HEADS]) * IDX_W_SCALE
            wi4 = jnp.pad(wi4, ((0, 0), (0, T_PAD - Tn), (0, 0)))
            w_col = wi4.transpose(0, 2, 1).reshape(Bd, N_IDX_HEADS * T_PAD, 1)
            score8 = _samp_score(page_table, ckit, j * n_pool, kitnew, qi_rows, w_col, Tn)
            s_tot = score8.shape[2]
            score_t = score8.transpose(2, 0, 1).reshape(s_tot, Bd * T_PAD)
            score_t = jnp.pad(score_t, ((0, 0), (0, LANES - Bd * T_PAD)))
            score_tiled = jnp.tile(score8.transpose(0, 2, 1), (1, 1, LANES // T_PAD))
            thr, cut = _samp_select(score_t, k_sel_s, Bd * T_PAD, Tn)
            per_seq = lambda a: jnp.tile(a[0, :Bd * T_PAD].reshape(Bd, 1, T_PAD), (1, 1, LANES // T_PAD))
            q4 = tb(q_bf).reshape(Bd, Tn, N_KV_HEADS, GROUP, HEAD_DIM)
            q4 = jnp.pad(q4, ((0, 0), (0, T_PAD - Tn), (0, 0), (0, 0), (0, 0)))
            qcat = q4.transpose(0, 4, 2, 3, 1).reshape(Bd, HEAD_DIM, N_HEADS * T_PAD)
            ot = _samp_attend(page_table, ck, cv, j * n_pool, knew, vnew, score_tiled,
                              per_seq(thr), per_seq(cut), qcat)
            o = ot.reshape(Bd, HEAD_DIM, N_HEADS, T_PAD)[..., :Tn]
            o = o.transpose(3, 0, 2, 1).reshape(Tn * Bd, N_HEADS * HEAD_DIM)
            xs, xs_bf = _mm_res_ln(o.astype(BF16), w_attn_out, j, zero_d, xs, ln_mix_g[i], ln_mix_b[i],
                                   alpha, gs.tm_down, 512)
            outs["ks"].append(tb(k_f).reshape(Bd, Tn, N_KV_HEADS, HEAD_DIM))
            outs["vs"].append(tb(v_f).reshape(Bd, Tn, N_KV_HEADS, HEAD_DIM))
            outs["kis"].append(tb(kiwi[:, :IDX_DIM]))
        else:
            a = _glu(xp_bf, w_pw1, b_pw1, j, gp.tm, 512)
            h_bf = _dwconv(a, None, w_dw[j], b_dw[j], ln_conv_g[j], ln_conv_b[j],
                           tm=min(256, T), stride=1, hist_rows=32, seq_tiles=T // min(256, T))
            xp, xp_bf = _mm_res_ln(h_bf, w_pw2, j, b_pw2[j], xp, ln_mix_g[i], ln_mix_b[i],
                                   alpha, gp.tm_down, 512)
            outs["convp"].append(a.reshape(B, T, D)[:, T - (cw - 1):])

            a = _glu(xs_bf, w_pw1, b_pw1, j, gs.tm, 512)
            h_bf = _dwconv(a, conv_hist_s[j], w_dw[j], b_dw[j], ln_conv_g[j], ln_conv_b[j],
                           tm=Tn * Bd, stride=Bd, hist_rows=(cw - 1) * Bd, seq_tiles=0)
            xs, xs_bf = _mm_res_ln(h_bf, w_pw2, j, b_pw2[j], xs, ln_mix_g[i], ln_mix_b[i],
                                   alpha, gs.tm_down, 512)
            a_pad = jnp.concatenate([conv_hist_s[j], a], axis=0)[Tn * Bd:]
            outs["convs"].append(a_pad.reshape(cw - 1, Bd, D).transpose(1, 0, 2))

        xp, xp_bf, g_tail = _ffn(gp, xp, xp_bf, i, ffn_hist_p, w_ffn_gate, w_ffn_up, w_ffn_conv,
                                 b_ffn_conv, w_ffn_down, ln_ffn_g, ln_ffn_b, alpha, tf)
        outs["ffnp"].append(g_tail.reshape(B, SUBLANES, F)[:, SUBLANES - 2:])
        xs, xs_bf, g_tail = _ffn(gs, xs, xs_bf, i, ffn_hist_s[i], w_ffn_gate, w_ffn_up, w_ffn_conv,
                                 b_ffn_conv, w_ffn_down, ln_ffn_g, ln_ffn_b, alpha, tf)
        outs["ffns"].append(g_tail.reshape(2, Bd, F).transpose(1, 0, 2))

    st = lambda name: jnp.stack(outs[name])
    y_prompt = xp.reshape(B, T, D)
    y_sample = xs.reshape(Tn, Bd, D).transpose(1, 0, 2)
    return (y_prompt, y_sample, st("kp"), st("vp"), st("kip"), st("convp"), st("ffnp"),
            st("ks"), st("vs"), st("kis"), st("convs"), st("ffns"))
```

```python
import functools

import jax
import jax.numpy as jnp
from jax import lax
from jax.experimental import pallas as pl
from jax.experimental.pallas import tpu as pltpu

F32 = jnp.float32
BF16 = jnp.bfloat16
I32 = jnp.int32

N_HEADS = 16
N_KV_HEADS = 4
HEAD_DIM = 128
GROUP = N_HEADS // N_KV_HEADS
ROT_DIM = HEAD_DIM // 4
ROPE_THETA = 500000.0
N_IDX_HEADS = 16
IDX_DIM = 64
IDX_ROT_DIM = IDX_DIM // 4
IDX_W_SCALE = (N_IDX_HEADS * IDX_DIM) ** -0.5
TOPK_MAX = 256
Q_BLOCK = 128
LN_EPS = 1e-5
LANES = 128
SUBLANES = 8
PAGES_PER_STEP = 8
T_PAD = 8
NEG_BIG = -1e30
INT_MIN = -(2 ** 31)
INT_MAX = 2 ** 31 - 1
NEG_INF_KEY = INT_MIN + 0x007FFFFF
VMEM_LIMIT = 56 * 2 ** 20


def _params(*sem):
    return pltpu.CompilerParams(dimension_semantics=sem, vmem_limit_bytes=VMEM_LIMIT)


def _dot(a, b):
    return jnp.dot(a, b, preferred_element_type=F32)


def _dot_nt(a, b):
    return lax.dot_general(a, b, (((1,), (1,)), ((), ())), preferred_element_type=F32)


def _layer_norm(z, g, b):
    mu = jnp.mean(z, axis=-1, keepdims=True)
    zc = z - mu
    var = jnp.mean(zc * zc, axis=-1, keepdims=True)
    return zc * lax.rsqrt(var + LN_EPS) * g + b


def _lane_tile(x, rep):
    return x if rep == 1 else jnp.concatenate([x] * rep, axis=1)


def _proj_body(*refs, shift):
    x_ref, w_ref = refs[0], refs[1]
    if shift:
        c_ref, s1_ref, s2_ref = refs[2:5]
        outs = refs[5:]
    else:
        outs = refs[2:]
    x = x_ref[...]
    tn = w_ref.shape[1]
    sub = min(tn, 256)
    rep = sub // LANES
    for c0 in range(0, tn, sub):
        y = _dot(x, w_ref[:, c0:c0 + sub].astype(BF16))
        if shift:
            y = (y * _lane_tile(c_ref[...], rep)
                 + pltpu.roll(y, shift, 1) * _lane_tile(s1_ref[...], rep)
                 + pltpu.roll(y, sub - shift, 1) * _lane_tile(s2_ref[...], rep))
        for o in outs:
            o[:, c0:c0 + sub] = y.astype(o.dtype)


def _proj(x_bf, w, layer, col_off, ncols, tabs, shift, out_dtypes, tm, tn):
    M, K = x_bf.shape
    assert col_off % tn == 0 and ncols % tn == 0 and M % tm == 0
    cb = col_off // tn
    in_specs = [pl.BlockSpec((tm, K), lambda i, j: (i, 0)),
                pl.BlockSpec((None, K, tn), lambda i, j: (layer, 0, cb + j))]
    args = [x_bf, w]
    if shift:
        period = tabs[0].shape[0] // tm
        for t in tabs:
            in_specs.append(pl.BlockSpec((tm, LANES), lambda i, j: (i % period, 0)))
            args.append(t)
    outs = pl.pallas_call(
        functools.partial(_proj_body, shift=shift),
        grid=(M // tm, ncols // tn),
        in_specs=in_specs,
        out_specs=[pl.BlockSpec((tm, tn), lambda i, j: (i, j)) for _ in out_dtypes],
        out_shape=[jax.ShapeDtypeStruct((M, ncols), dt) for dt in out_dtypes],
        compiler_params=_params("parallel", "arbitrary"),
        name="proj",
    )(*args)
    return outs


def _glu_body(x_ref, wa_ref, wg_ref, ba_ref, bg_ref, a_ref):
    x = x_ref[...]
    tn = wa_ref.shape[1]
    sub = min(tn, 256)
    for c0 in range(0, tn, sub):
        cols = slice(c0, c0 + sub)
        a = _dot(x, wa_ref[:, cols].astype(BF16)) + ba_ref[:, cols]
        g = _dot(x, wg_ref[:, cols].astype(BF16)) + bg_ref[:, cols]
        a_ref[:, cols] = a * jax.nn.sigmoid(g)


def _glu(x_bf, w_pw1, b_pw1, layer, tm, tn):
    M, K = x_bf.shape
    C = w_pw1.shape[2] // 2
    nb = C // tn
    b2 = b_pw1[layer].reshape(1, 2 * C)
    return pl.pallas_call(
        _glu_body,
        grid=(M // tm, nb),
        in_specs=[pl.BlockSpec((tm, K), lambda i, j: (i, 0)),
                  pl.BlockSpec((None, K, tn), lambda i, j: (layer, 0, j)),
                  pl.BlockSpec((None, K, tn), lambda i, j: (layer, 0, nb + j)),
                  pl.BlockSpec((1, tn), lambda i, j: (0, j)),
                  pl.BlockSpec((1, tn), lambda i, j: (0, nb + j))],
        out_specs=pl.BlockSpec((tm, tn), lambda i, j: (i, j)),
        out_shape=jax.ShapeDtypeStruct((M, C), F32),
        compiler_params=_params("parallel", "arbitrary"),
        name="glu",
    )(x_bf, w_pw1, w_pw1, b2, b2)


def _dwconv_body(a_ref, hist_ref, w_ref, b_ref, g_ref, beta_ref, h_ref, ext_ref, conv_ref, sh_ref,
                 *, stride, hist_rows, width, seq_tiles, row_block):
    tm, C = a_ref.shape
    R = hist_rows + tm
    hist = hist_ref[...]
    if seq_tiles:
        hist = jnp.where(pl.program_id(0) % seq_tiles == 0, 0.0, hist)
    ext_ref[0:hist_rows, :] = hist
    ext_ref[hist_rows:R, :] = a_ref[...]
    tap_off = [hist_rows - (width - 1 - k) * stride for k in range(width)]
    residues = sorted({off % SUBLANES for off in tap_off} - {0})

    def chunk(ci, carry):
        c0 = pl.multiple_of(ci * LANES, LANES)
        wch = w_ref[:, pl.ds(c0, LANES)]
        bch = b_ref[:, pl.ds(c0, LANES)]
        for r in residues:
            sh_ref[r, 0:R - SUBLANES, :] = ext_ref[pl.ds(r, R - SUBLANES), pl.ds(c0, LANES)]
        for r0 in range(0, tm, row_block):
            acc = jnp.broadcast_to(bch, (row_block, LANES))
            for k in range(width):
                r = tap_off[k] % SUBLANES
                base = tap_off[k] - r + r0
                if r == 0:
                    tap = ext_ref[pl.ds(base, row_block), pl.ds(c0, LANES)]
                else:
                    tap = sh_ref[r, pl.ds(base, row_block), :]
                acc = acc + tap * wch[k:k + 1, :]
            conv_ref[pl.ds(r0, row_block), pl.ds(c0, LANES)] = acc
        return carry

    lax.fori_loop(0, C // LANES, chunk, 0)
    y = _layer_norm(conv_ref[...], g_ref[...], beta_ref[...])
    h_ref[...] = (y * jax.nn.sigmoid(y)).astype(h_ref.dtype)


def _dwconv(a, hist, w_dw, b_dw, ln_g, ln_b, *, tm, stride, hist_rows, seq_tiles):
    M, C = a.shape
    width = w_dw.shape[0]
    if seq_tiles:
        per = tm // hist_rows
        hist_arr = a
        hist_spec = pl.BlockSpec((hist_rows, C), lambda i: (jnp.maximum(i * per - 1, 0), 0))
    else:
        hist_arr = hist
        hist_spec = pl.BlockSpec((hist_rows, C), lambda i: (0, 0))
    vec = lambda: pl.BlockSpec((1, C), lambda i: (0, 0))
    return pl.pallas_call(
        functools.partial(_dwconv_body, stride=stride, hist_rows=hist_rows, width=width,
                          seq_tiles=seq_tiles, row_block=min(tm, 64)),
        grid=(M // tm,),
        in_specs=[pl.BlockSpec((tm, C), lambda i: (i, 0)), hist_spec,
                  pl.BlockSpec((width, C), lambda i: (0, 0)), vec(), vec(), vec()],
        out_specs=pl.BlockSpec((tm, C), lambda i: (i, 0)),
        out_shape=jax.ShapeDtypeStruct((M, C), BF16),
        scratch_shapes=[pltpu.VMEM((hist_rows + tm, C), F32), pltpu.VMEM((tm, C), F32),
                        pltpu.VMEM((SUBLANES, hist_rows + tm, LANES), F32)],
        compiler_params=_params("arbitrary"),
        name="dwconv",
    )(a, hist_arr, w_dw, b_dw.reshape(1, C), ln_g.reshape(1, C), ln_b.reshape(1, C))


def _mm_res_ln_body(a_ref, w_ref, bias_ref, res_ref, g_ref, b_ref, xo_ref, xb_ref, *, nk, alpha):
    k = pl.program_id(1)

    @pl.when(k == 0)
    def _():
        xo_ref[...] = jnp.zeros(xo_ref.shape, F32)

    a = a_ref[...]
    N = xo_ref.shape[1]
    nc = min(N, 512)
    for c0 in range(0, N, nc):
        xo_ref[:, c0:c0 + nc] += _dot(a, w_ref[:, c0:c0 + nc].astype(BF16))

    @pl.when(k == nk - 1)
    def _():
        z = alpha * res_ref[...] + (xo_ref[...] + bias_ref[...])
        out = _layer_norm(z, g_ref[...], b_ref[...])
        xo_ref[...] = out
        xb_ref[...] = out.astype(BF16)


def _mm_res_ln(a_bf, w, layer, bias, res, ln_g, ln_b, alpha, tm, tk):
    M, K = a_bf.shape
    N = w.shape[2]
    nk = K // tk
    vec = lambda: pl.BlockSpec((1, N), lambda i, k: (0, 0))
    return pl.pallas_call(
        functools.partial(_mm_res_ln_body, nk=nk, alpha=alpha),
        grid=(M // tm, nk),
        in_specs=[pl.BlockSpec((tm, tk), lambda i, k: (i, k)),
                  pl.BlockSpec((None, tk, N), lambda i, k: (layer, k, 0)),
                  vec(),
                  pl.BlockSpec((tm, N), lambda i, k: (i, 0), pipeline_mode=pl.Buffered(1)),
                  vec(), vec()],
        out_specs=[pl.BlockSpec((tm, N), lambda i, j: (i, 0)),
                   pl.BlockSpec((tm, N), lambda i, j: (i, 0))],
        out_shape=[jax.ShapeDtypeStruct((M, N), F32), jax.ShapeDtypeStruct((M, N), BF16)],
        compiler_params=_params("parallel", "arbitrary"),
        name="mm_res_ln",
    )(a_bf, w, bias.reshape(1, N), res, ln_g.reshape(1, N), ln_b.reshape(1, N))


def _ffn_up_body(x_ref, wg_ref, wu_ref, hist_ref, wc_ref, bc_ref, h_ref, gt_ref, ext_ref,
                 *, stride, hist_rows, sub):
    x = x_ref[...]
    tm = x.shape[0]
    H = hist_rows
    tf = wg_ref.shape[1]
    for c0 in range(0, tf, sub):
        cols = slice(c0, c0 + sub)
        g = _dot(x, wg_ref[:, cols].astype(BF16))
        ext_ref[0:H, cols] = hist_ref[:, cols]
        ext_ref[H:H + tm, cols] = g
        gt_ref[:, cols] = g[tm - H:tm, :]
        u = _dot(x, wu_ref[:, cols].astype(BF16))
        wc = wc_ref[:, cols]
        c = (bc_ref[:, cols] + wc[2:3, :] * g
             + wc[1:2, :] * ext_ref[H - stride:H - stride + tm, cols]
             + wc[0:1, :] * ext_ref[H - 2 * stride:H - 2 * stride + tm, cols])
        h_ref[:, cols] = (c * jax.nn.sigmoid(c) * u).astype(h_ref.dtype)


def _ffn_up(x_bf, w_gate, w_up, layer, hist, w_c, b_c, *, tm, tf, stride, hist_rows):
    M, K = x_bf.shape
    F = w_gate.shape[2]
    H = hist_rows
    return pl.pallas_call(
        functools.partial(_ffn_up_body, stride=stride, hist_rows=H, sub=min(tf, 256)),
        grid=(M // tm, F // tf),
        in_specs=[pl.BlockSpec((tm, K), lambda i, j: (i, 0)),
                  pl.BlockSpec((None, K, tf), lambda i, j: (layer, 0, j)),
                  pl.BlockSpec((None, K, tf), lambda i, j: (layer, 0, j)),
                  pl.BlockSpec((H, tf), lambda i, j: (i, j)),
                  pl.BlockSpec((3, tf), lambda i, j: (0, j)),
                  pl.BlockSpec((1, tf), lambda i, j: (0, j))],
        out_specs=[pl.BlockSpec((tm, tf), lambda i, j: (i, j)),
                   pl.BlockSpec((H, tf), lambda i, j: (i, j))],
        out_shape=[jax.ShapeDtypeStruct((M, F), BF16),
                   jax.ShapeDtypeStruct((M // tm * H, F), F32)],
        scratch_shapes=[pltpu.VMEM((H + tm, tf), F32)],
        compiler_params=_params("parallel", "arbitrary"),
        name="ffn_up",
    )(x_bf, w_gate, w_up, hist, w_c, b_c.reshape(1, F))


def _order_key(score):
    bits = pltpu.bitcast(score + 0.0, I32)
    return jnp.where(bits < 0, bits ^ INT_MAX, bits)


COUNT_CHAINS = 8


def _count(mask):
    S, L = mask.shape
    assert S < 2 ** 24
    x = jnp.where(mask, 1.0, 0.0).astype(F32)
    group = COUNT_CHAINS * SUBLANES
    if S % group == 0 and S > group:
        chains = [x[c * SUBLANES:(c + 1) * SUBLANES] for c in range(COUNT_CHAINS)]
        for r in range(group, S, group):
            for c in range(COUNT_CHAINS):
                chains[c] = chains[c] + x[r + c * SUBLANES:r + (c + 1) * SUBLANES]
        while len(chains) > 1:
            chains = [chains[i] + chains[i + 1] for i in range(0, len(chains), 2)]
        x = chains[0]
    return jnp.sum(x, axis=0, keepdims=True).astype(I32)


def _select_topk(key_ref, cut_ref, k, lane_ok):
    S, L = key_ref.shape
    cnt0 = _count(key_ref[...] >= 0)
    base = jnp.where(cnt0 >= k, 0, INT_MIN).astype(I32)

    def bit_step(t, base):
        cand = base | lax.shift_left(jnp.int32(1), 30 - t)
        return jnp.where(_count(key_ref[...] >= cand) >= k, cand, base)

    thr = lax.fori_loop(0, 31, bit_step, base)
    key = key_ref[...]
    n_gt = _count(key > thr)
    n_ge = _count(key >= thr)
    need = k - n_gt
    tie = (n_ge > k) & (thr > NEG_INF_KEY) & lane_ok
    cut_ref[...] = jnp.full((1, L), INT_MAX, I32)
    n_tie = jnp.sum(jnp.where(tie, 1, 0).astype(I32))

    @pl.when(n_tie > 0)
    def _():
        nbits = max(1, (S - 1).bit_length())
        rows = lax.broadcasted_iota(I32, (S, L), 0)

        def idx_step(t, pos):
            cand = pos | lax.shift_left(jnp.int32(1), nbits - 1 - t)
            c = _count((key_ref[...] == thr) & (rows < cand))
            return jnp.where(c < need, cand, pos)

        cut_ref[...] = lax.fori_loop(0, nbits, idx_step, jnp.zeros((1, L), I32))

    return thr


def _attn_prompt_body(qi_ref, wit_ref, kiwi_ref, q_ref, k_ref, v_ref, o_ref,
                      ki2_ref, sc_ref, key_ref, cut_ref, vt_ref, *, k_sel, n_var):
    S = kiwi_ref.shape[0]
    tq = qi_ref.shape[0]
    qb = pl.program_id(1)

    @pl.when(qb == 0)
    def _():
        lane = lax.broadcasted_iota(I32, (S, LANES), 1)
        lo = jnp.where(lane < IDX_DIM, kiwi_ref[...], 0.0)
        ki2_ref[0:S, :] = lo.astype(BF16)
        ki2_ref[S:2 * S, :] = pltpu.roll(lo, IDX_DIM, 1).astype(BF16)
        for n in range(N_KV_HEADS):
            rows = slice(n * HEAD_DIM, (n + 1) * HEAD_DIM)
            vt_ref[rows, :] = v_ref[:, rows].T.astype(BF16)

    ws = wit_ref[...] * IDX_W_SCALE
    scale = HEAD_DIM ** -0.5

    def attend(su):
        sc = sc_ref.at[0:su]
        keys = key_ref.at[0:su]
        for p in range(N_IDX_HEADS // 2):
            qp = qi_ref[:, p * LANES:(p + 1) * LANES]
            d0 = _dot_nt(ki2_ref[0:su, :], qp)
            d1 = _dot_nt(ki2_ref[S:S + su, :], qp)
            part = (ws[2 * p:2 * p + 1, :] * jnp.maximum(d0, 0.0)
                    + ws[2 * p + 1:2 * p + 2, :] * jnp.maximum(d1, 0.0))
            if p == 0:
                sc[...] = part
            else:
                sc[...] += part

        rows = lax.broadcasted_iota(I32, (su, tq), 0)
        qpos = qb * tq + lax.broadcasted_iota(I32, (su, tq), 1)
        causal = rows <= qpos
        keys[...] = _order_key(jnp.where(causal, sc[...], -jnp.inf))
        thr = _select_topk(keys, cut_ref, k_sel, jnp.full((1, tq), True))
        key = keys[...]
        sel = ((key > thr) | ((key == thr) & (rows <= cut_ref[...]))) & causal
        bias = _lane_tile(jnp.where(sel, 0.0, NEG_BIG), GROUP)

        for n in range(N_KV_HEADS):
            qs = jnp.concatenate([q_ref[:, (n * GROUP + g) * HEAD_DIM:(n * GROUP + g + 1) * HEAD_DIM]
                                  for g in range(GROUP)], axis=0)
            s = _dot_nt(k_ref[0:su, n * HEAD_DIM:(n + 1) * HEAD_DIM], qs) + bias
            m = jnp.max(s, axis=0, keepdims=True)
            e = jnp.exp((s - m) * scale)
            l = jnp.sum(e, axis=0, keepdims=True)
            ot = _dot(vt_ref[n * HEAD_DIM:(n + 1) * HEAD_DIM, 0:su], e.astype(BF16)) * (1.0 / l)
            for g in range(GROUP):
                h = n * GROUP + g
                o_ref[:, h * HEAD_DIM:(h + 1) * HEAD_DIM] = (
                    ot[:, g * tq:(g + 1) * tq].T.astype(o_ref.dtype))

    nb = S // tq
    per = nb // n_var
    for c in range(n_var):
        pl.when(qb // per == c)(functools.partial(attend, (c + 1) * per * tq))


def _attn_prompt(qi_bf, wit, kiwi, q_bf, k_bf, v_f, B, T, k_sel):
    tq = Q_BLOCK
    nb = T // tq
    n_var = 4 if nb % 4 == 0 else 1
    QC = N_HEADS * HEAD_DIM
    KC = N_KV_HEADS * HEAD_DIM
    QI = N_IDX_HEADS * IDX_DIM
    return pl.pallas_call(
        functools.partial(_attn_prompt_body, k_sel=k_sel, n_var=n_var),
        grid=(B, nb),
        in_specs=[pl.BlockSpec((tq, QI), lambda b, i: (b * nb + i, 0)),
                  pl.BlockSpec((N_IDX_HEADS, tq), lambda b, i: (0, b * nb + i)),
                  pl.BlockSpec((T, LANES), lambda b, i: (b, 0)),
                  pl.BlockSpec((tq, QC), lambda b, i: (b * nb + i, 0)),
                  pl.BlockSpec((T, KC), lambda b, i: (b, 0)),
                  pl.BlockSpec((T, KC), lambda b, i: (b, 0))],
        out_specs=pl.BlockSpec((tq, QC), lambda b, i: (b * nb + i, 0)),
        out_shape=jax.ShapeDtypeStruct((B * T, QC), BF16),
        scratch_shapes=[pltpu.VMEM((2 * T, LANES), BF16), pltpu.VMEM((T, tq), F32),
                        pltpu.VMEM((T, tq), I32), pltpu.VMEM((1, tq), I32),
                        pltpu.VMEM((KC, T), BF16)],
        compiler_params=_params("parallel", "arbitrary"),
        name="attn_prompt",
    )(qi_bf, wit, kiwi, q_bf, k_bf, v_f)


def _page_scores(kit, qi, w_col):
    d = _dot(qi, kit.astype(BF16))
    z = jnp.maximum(d, 0.0) * w_col
    return z.reshape(N_IDX_HEADS, T_PAD, LANES).sum(axis=0)


def _samp_score_body(pt_ref, *refs, n_tok):
    G = PAGES_PER_STEP
    pages = refs[:G]
    kinew_ref, qi_ref, wc_ref, out_ref = refs[G:G + 4]
    p = pl.program_id(1)
    n_steps = pl.num_programs(1)
    qi = qi_ref[...]
    wc = wc_ref[...]

    @pl.when(p < n_steps - 1)
    def _():
        for g in range(G):
            out_ref[:, g * LANES:(g + 1) * LANES] = _page_scores(pages[g][...], qi, wc)

    @pl.when(p == n_steps - 1)
    def _():
        z = _page_scores(kinew_ref[...], qi, wc)
        tok = lax.broadcasted_iota(I32, (T_PAD, LANES), 0)
        new = lax.broadcasted_iota(I32, (T_PAD, LANES), 1)
        ok = (new <= tok) & (new < n_tok)
        out_ref[:, 0:LANES] = jnp.where(ok, z, -jnp.inf)
        out_ref[:, LANES:G * LANES] = jnp.full((T_PAD, (G - 1) * LANES), -jnp.inf, F32)


def _page_index_map(g, layer_off, n_steps, rank=3):
    def index_map(b, p, pt):
        pg = jnp.minimum(p, n_steps - 2) * PAGES_PER_STEP + g % PAGES_PER_STEP
        return (layer_off + pt[b, pg],) + (0,) * (rank - 1)
    return index_map


def _samp_score(page_table, cache_kit, layer_off, kitnew, qi_rows, w_col, n_tok):
    Bd, n_pages = page_table.shape
    G = PAGES_PER_STEP
    n_steps = n_pages // G + 1
    page = cache_kit.shape[2]
    in_specs = [pl.BlockSpec((None, IDX_DIM, page), _page_index_map(g, layer_off, n_steps))
                for g in range(G)]
    in_specs += [pl.BlockSpec((None, IDX_DIM, LANES), lambda b, p, pt: (b, 0, 0)),
                 pl.BlockSpec((None, LANES, IDX_DIM), lambda b, p, pt: (b, 0, 0)),
                 pl.BlockSpec((None, LANES, 1), lambda b, p, pt: (b, 0, 0))]
    return pl.pallas_call(
        functools.partial(_samp_score_body, n_tok=n_tok),
        grid_spec=pltpu.PrefetchScalarGridSpec(
            num_scalar_prefetch=1, grid=(Bd, n_steps), in_specs=in_specs,
            out_specs=pl.BlockSpec((None, T_PAD, G * page), lambda b, p, pt: (b, 0, p))),
        out_shape=jax.ShapeDtypeStruct((Bd, T_PAD, n_steps * G * page), F32),
        compiler_params=_params("parallel", "arbitrary"),
        name="samp_score",
    )(page_table, *([cache_kit] * G), kitnew, qi_rows, w_col)


def _samp_select_body(sc_ref, thr_ref, cut_ref, key_ref, *, k_sel, n_lanes, n_tok):
    key_ref[...] = _order_key(sc_ref[...])
    lane = lax.broadcasted_iota(I32, (1, LANES), 1)
    lane_ok = (lane < n_lanes) & (lane % T_PAD < n_tok)
    thr_ref[...] = _select_topk(key_ref, cut_ref, k_sel, lane_ok)


def _samp_select(score_t, k_sel, n_lanes, n_tok):
    S = score_t.shape[0]
    return pl.pallas_call(
        functools.partial(_samp_select_body, k_sel=k_sel, n_lanes=n_lanes, n_tok=n_tok),
        out_shape=[jax.ShapeDtypeStruct((1, LANES), I32), jax.ShapeDtypeStruct((1, LANES), I32)],
        scratch_shapes=[pltpu.VMEM((S, LANES), I32)],
        compiler_params=pltpu.CompilerParams(vmem_limit_bytes=VMEM_LIMIT),
        name="samp_select",
    )(score_t)


def _samp_attend_body(pt_ref, *refs):
    G = PAGES_PER_STEP
    kp = refs[:G]
    vp = refs[G:2 * G]
    (knew_ref, vnew_ref, sc_ref, thr_ref, cut_ref, qcat_ref, o_ref,
     m_ref, l_ref, acc_ref) = refs[2 * G:]
    p = pl.program_id(1)
    n_steps = pl.num_programs(1)
    scale = HEAD_DIM ** -0.5
    R = knew_ref.shape[0]

    @pl.when(p == 0)
    def _():
        m_ref[...] = jnp.full(m_ref.shape, NEG_BIG, F32)
        l_ref[...] = jnp.zeros(l_ref.shape, F32)
        acc_ref[...] = jnp.zeros(acc_ref.shape, F32)

    thr = thr_ref[...]
    cut = cut_ref[...]
    qcat = qcat_ref[...]
    rr = lax.broadcasted_iota(I32, (R, LANES), 0)
    cc = lax.broadcasted_iota(I32, (R, LANES), 1)
    head_bias = jnp.where(rr % N_KV_HEADS == cc // (GROUP * T_PAD), 0.0, NEG_BIG)
    rep = jnp.where(rr // N_KV_HEADS == cc, 1.0, 0.0).astype(BF16)

    def page_logits(k_ref, g):
        s = _dot(k_ref[...].astype(BF16), qcat) * scale
        key = _order_key(sc_ref[g * LANES:(g + 1) * LANES, :])
        row = (p * G + g) * LANES + lax.broadcasted_iota(I32, (LANES, LANES), 0)
        sel = (key > thr) | ((key == thr) & (row <= cut))
        bias = jnp.where(sel, 0.0, NEG_BIG).astype(BF16)
        return s + _dot(rep, bias) + head_bias

    def fold(k_refs, v_refs):
        logits = [page_logits(k_ref, g) for g, k_ref in enumerate(k_refs)]
        m_old = m_ref[...]
        m_new = m_old
        for s in logits:
            m_new = jnp.maximum(m_new, jnp.max(s, axis=0, keepdims=True))
        corr = jnp.exp(m_old - m_new)
        l_new = l_ref[...] * corr
        acc = acc_ref[...] * corr
        for s, v_ref in zip(logits, v_refs):
            e = jnp.exp(s - m_new)
            l_new = l_new + jnp.sum(e, axis=0, keepdims=True)
            acc = acc + _dot(v_ref[...].T.astype(BF16), e.astype(BF16))
        m_ref[...] = m_new
        l_ref[...] = l_new
        acc_ref[...] = acc

    @pl.when(p < n_steps - 1)
    def _():
        fold(kp, vp)

    @pl.when(p == n_steps - 1)
    def _():
        fold([knew_ref], [vnew_ref])
        o_ref[...] = acc_ref[...] * (1.0 / l_ref[...])


def _samp_attend(page_table, cache_k, cache_v, layer_off, knew, vnew, score_tiled, thr_l, cut_l, qcat):
    Bd, n_pages = page_table.shape
    G = PAGES_PER_STEP
    n_steps = n_pages // G + 1
    R = cache_k.shape[1]
    page = R // N_KV_HEADS
    in_specs = [pl.BlockSpec((None, R, HEAD_DIM), _page_index_map(g, layer_off, n_steps))
                for g in range(2 * G)]
    per_seq = lambda r, c: pl.BlockSpec((None, r, c), lambda b, p, pt: (b, 0, 0))
    in_specs += [per_seq(R, HEAD_DIM), per_seq(R, HEAD_DIM),
                 pl.BlockSpec((None, G * page, LANES), lambda b, p, pt: (b, p, 0)),
                 per_seq(1, LANES), per_seq(1, LANES), per_seq(HEAD_DIM, LANES)]
    return pl.pallas_call(
        _samp_attend_body,
        grid_spec=pltpu.PrefetchScalarGridSpec(
            num_scalar_prefetch=1, grid=(Bd, n_steps), in_specs=in_specs,
            out_specs=pl.BlockSpec((None, HEAD_DIM, LANES), lambda b, p, pt: (b, 0, 0)),
            scratch_shapes=[pltpu.VMEM((1, LANES), F32), pltpu.VMEM((1, LANES), F32),
                            pltpu.VMEM((HEAD_DIM, LANES), F32)]),
        out_shape=jax.ShapeDtypeStruct((Bd, HEAD_DIM, LANES), F32),
        compiler_params=_params("parallel", "arbitrary"),
        name="samp_attend",
    )(page_table, *([cache_k] * G), *([cache_v] * G), knew, vnew, score_tiled, thr_l, cut_l, qcat)


def _rope_tables(pos, rot_dim, period):
    half = rot_dim // 2
    inv = ROPE_THETA ** (-jnp.arange(0, rot_dim, 2, dtype=F32) / rot_dim)
    ang = pos.astype(F32)[:, None] * inv[None, :]
    cos, sin = jnp.cos(ang), jnp.sin(ang)
    lane = jnp.arange(LANES) % period
    fi = lane % half
    c = jnp.where(lane[None, :] < rot_dim, cos[:, fi], 1.0)
    s_hi = jnp.where((lane[None, :] >= half) & (lane[None, :] < rot_dim), sin[:, fi], 0.0)
    s_lo = jnp.where(lane[None, :] < half, -sin[:, fi], 0.0)
    return c, s_hi, s_lo


class _Group:
    def __init__(self, rows, seq_rows, stride, pos, tm, tm_down):
        self.rows = rows
        self.seq_rows = seq_rows
        self.stride = stride
        self.tm = tm
        self.tm_down = tm_down
        self.tab_q = _rope_tables(pos, ROT_DIM, HEAD_DIM)
        self.tab_i = _rope_tables(pos, IDX_ROT_DIM, IDX_DIM)


def _in_proj(grp, x_bf, w_in, j):
    QC = N_HEADS * HEAD_DIM
    KC = N_KV_HEADS * HEAD_DIM
    QI = N_IDX_HEADS * IDX_DIM
    tm = grp.tm
    (q_bf,) = _proj(x_bf, w_in, j, 0, QC, grp.tab_q, ROT_DIM // 2, [BF16], tm, 512)
    k_f, k_bf = _proj(x_bf, w_in, j, QC, KC, grp.tab_q, ROT_DIM // 2, [F32, BF16], tm, 512)
    (v_f,) = _proj(x_bf, w_in, j, QC + KC, KC, None, 0, [F32], tm, 512)
    (qi_bf,) = _proj(x_bf, w_in, j, QC + 2 * KC, QI, grp.tab_i, IDX_ROT_DIM // 2, [BF16], tm, 512)
    c, s_hi, s_lo = grp.tab_i
    lane = jnp.arange(LANES)[None, :]
    keep = lane < IDX_DIM
    tab_kw = (jnp.where(keep, c, 1.0), jnp.where(keep, s_hi, 0.0), jnp.where(keep, s_lo, 0.0))
    n_tail = w_in.shape[2] - (QC + 2 * KC + QI)
    w_tail = jnp.pad(w_in[j:j + 1, :, QC + 2 * KC + QI:], ((0, 0), (0, 0), (0, LANES - n_tail)))
    (kiwi,) = _proj(x_bf, w_tail, 0, 0, LANES, tab_kw, IDX_ROT_DIM // 2, [F32], tm, LANES)
    return q_bf, k_f, k_bf, v_f, qi_bf, kiwi


def _ffn(grp, x, x_bf, i, hist, w_gate, w_up, w_c, b_c, w_down, ln_g, ln_b, alpha, tf):
    F = w_gate.shape[2]
    H = hist.shape[0] // (grp.rows // grp.seq_rows)
    h_bf, g_tail = _ffn_up(x_bf, w_gate, w_up, i, hist, w_c[i], b_c[i],
                           tm=grp.seq_rows, tf=tf, stride=grp.stride, hist_rows=H)
    x, x_bf = _mm_res_ln(h_bf, w_down, i, jnp.zeros((w_down.shape[2],), F32), x, ln_g[i], ln_b[i],
                         alpha, grp.tm_down, 512)
    return x, x_bf, g_tail


def kernel(x_prompt, x_sample, cache_k, cache_v, cache_kidx, state_conv, state_ffn, page_table,
           w_attn_in, w_attn_out, w_pw1, b_pw1, w_dw, b_dw, ln_conv_g, ln_conv_b, w_pw2, b_pw2,
           w_ffn_gate, w_ffn_up, w_ffn_conv, b_ffn_conv, w_ffn_down,
           ln_mix_g, ln_mix_b, ln_ffn_g, ln_ffn_b):
    B, T, D = x_prompt.shape
    Bd, Tn, _ = x_sample.shape
    depth = w_ffn_gate.shape[0]
    F = w_ffn_gate.shape[2]
    n_attn, n_pool, page = cache_k.shape[:3]
    n_conv = w_pw1.shape[0]
    cw = w_dw.shape[1]
    past = page_table.shape[1] * page
    KC = N_KV_HEADS * HEAD_DIM
    alpha = float((2 * depth) ** 0.25)
    assert Bd == SUBLANES and Tn <= T_PAD and D == N_HEADS * HEAD_DIM
    tf = 512 if F % 512 == 0 else LANES

    gp = _Group(B * T, T, 1, jnp.arange(T, dtype=I32), min(1024, T), min(1024, T))
    pos_s = past + jnp.arange(Tn, dtype=I32)
    gs = _Group(Tn * Bd, Tn * Bd, Bd, jnp.repeat(pos_s, Bd), Tn * Bd, Tn * Bd)

    xp = x_prompt.reshape(B * T, D)
    xs = x_sample.transpose(1, 0, 2).reshape(Tn * Bd, D)
    xp_bf, xs_bf = xp.astype(BF16), xs.astype(BF16)

    ck = cache_k.reshape(n_attn * n_pool, page * N_KV_HEADS, HEAD_DIM)
    cv = cache_v.reshape(n_attn * n_pool, page * N_KV_HEADS, HEAD_DIM)
    ckit = cache_kidx.reshape(n_attn * n_pool, page, IDX_DIM).transpose(0, 2, 1)
    conv_hist_s = state_conv.transpose(0, 2, 1, 3).reshape(n_conv, (cw - 1) * Bd, D)
    ffn_hist_s = state_ffn.transpose(0, 2, 1, 3).reshape(depth, 2 * Bd, F)
    ffn_hist_p = jnp.zeros((B * SUBLANES, F), F32)
    zero_d = jnp.zeros((D,), F32)

    k_sel_p = min(TOPK_MAX, T // 4)
    k_sel_s = min(TOPK_MAX, (past + Tn) // 4)

    outs = {name: [] for name in ("kp", "vp", "kip", "convp", "ffnp", "ks", "vs", "kis", "convs", "ffns")}

    for i in range(depth):
        j = i // 2
        if i % 2 == 0:
            q_bf, k_f, k_bf, v_f, qi_bf, kiwi = _in_proj(gp, xp_bf, w_attn_in, j)
            wit = kiwi[:, IDX_DIM:IDX_DIM + N_IDX_HEADS].T
            o_bf = _attn_prompt(qi_bf, wit, kiwi, q_bf, k_bf, v_f, B, T, k_sel_p)
            xp, xp_bf = _mm_res_ln(o_bf, w_attn_out, j, zero_d, xp, ln_mix_g[i], ln_mix_b[i],
                                   alpha, gp.tm_down, 512)
            outs["kp"].append(k_f.reshape(B, T, N_KV_HEADS, HEAD_DIM))
            outs["vp"].append(v_f.reshape(B, T, N_KV_HEADS, HEAD_DIM))
            outs["kip"].append(kiwi[:, :IDX_DIM].reshape(B, T, IDX_DIM))

            q_bf, k_f, k_bf, v_f, qi_bf, kiwi = _in_proj(gs, xs_bf, w_attn_in, j)
            tb = lambda a: a.reshape(Tn, Bd, -1).transpose(1, 0, 2)
            pad_rows = lambda a: jnp.pad(a, ((0, 0), (0, LANES - Tn), (0, 0)))
            kitnew = pad_rows(tb(kiwi[:, :IDX_DIM])).transpose(0, 2, 1)
            heads = lambda a: a.reshape(Bd, LANES * N_KV_HEADS, HEAD_DIM)
            knew, vnew = heads(pad_rows(tb(k_f))), heads(pad_rows(tb(v_f)))
            qi4 = tb(qi_bf).reshape(Bd, Tn, N_IDX_HEADS, IDX_DIM)
            qi4 = jnp.pad(qi4, ((0, 0), (0, T_PAD - Tn), (0, 0), (0, 0)))
            qi_rows = qi4.transpose(0, 2, 1, 3).reshape(Bd, N_IDX_HEADS * T_PAD, IDX_DIM)
            wi4 = tb(kiwi[:, IDX_DIM:IDX_DIM + N_IDX_HEADS]) * IDX_W_SCALE
            wi4 = jnp.pad(wi4, ((0, 0), (0, T_PAD - Tn), (0, 0)))
            w_col = wi4.transpose(0, 2, 1).reshape(Bd, N_IDX_HEADS * T_PAD, 1)
            score8 = _samp_score(page_table, ckit, j * n_pool, kitnew, qi_rows, w_col, Tn)
            s_tot = score8.shape[2]
            score_t = score8.transpose(2, 0, 1).reshape(s_tot, Bd * T_PAD)
            score_t = jnp.pad(score_t, ((0, 0), (0, LANES - Bd * T_PAD)))
            score_tiled = jnp.tile(score8.transpose(0, 2, 1), (1, 1, LANES // T_PAD))
            thr, cut = _samp_select(score_t, k_sel_s, Bd * T_PAD, Tn)
            per_seq = lambda a: jnp.tile(a[0, :Bd * T_PAD].reshape(Bd, 1, T_PAD), (1, 1, LANES // T_PAD))
            q4 = tb(q_bf).reshape(Bd, Tn, N_KV_HEADS, GROUP, HEAD_DIM)
            q4 = jnp.pad(q4, ((0, 0), (0, T_PAD - Tn), (0, 0), (0, 0), (0, 0)))
            qcat = q4.transpose(0, 4, 2, 3, 1).reshape(Bd, HEAD_DIM, N_HEADS * T_PAD)
            ot = _samp_attend(page_table, ck, cv, j * n_pool, knew, vnew, score_tiled,
                              per_seq(thr), per_seq(cut), qcat)
            o = ot.reshape(Bd, HEAD_DIM, N_HEADS, T_PAD)[..., :Tn]
            o = o.transpose(3, 0, 2, 1).reshape(Tn * Bd, N_HEADS * HEAD_DIM)
            xs, xs_bf = _mm_res_ln(o.astype(BF16), w_attn_out, j, zero_d, xs, ln_mix_g[i], ln_mix_b[i],
                                   alpha, gs.tm_down, 512)
            outs["ks"].append(tb(k_f).reshape(Bd, Tn, N_KV_HEADS, HEAD_DIM))
            outs["vs"].append(tb(v_f).reshape(Bd, Tn, N_KV_HEADS, HEAD_DIM))
            outs["kis"].append(tb(kiwi[:, :IDX_DIM]))
        else:
            a = _glu(xp_bf, w_pw1, b_pw1, j, gp.tm, 512)
            h_bf = _dwconv(a, None, w_dw[j], b_dw[j], ln_conv_g[j], ln_conv_b[j],
                           tm=min(256, T), stride=1, hist_rows=32, seq_tiles=T // min(256, T))
            xp, xp_bf = _mm_res_ln(h_bf, w_pw2, j, b_pw2[j], xp, ln_mix_g[i], ln_mix_b[i],
                                   alpha, gp.tm_down, 512)
            outs["convp"].append(a.reshape(B, T, D)[:, T - (cw - 1):])

            a = _glu(xs_bf, w_pw1, b_pw1, j, gs.tm, 512)
            h_bf = _dwconv(a, conv_hist_s[j], w_dw[j], b_dw[j], ln_conv_g[j], ln_conv_b[j],
                           tm=Tn * Bd, stride=Bd, hist_rows=(cw - 1) * Bd, seq_tiles=0)
            xs, xs_bf = _mm_res_ln(h_bf, w_pw2, j, b_pw2[j], xs, ln_mix_g[i], ln_mix_b[i],
                                   alpha, gs.tm_down, 512)
            a_pad = jnp.concatenate([conv_hist_s[j], a], axis=0)[Tn * Bd:]
            outs["convs"].append(a_pad.reshape(cw - 1, Bd, D).transpose(1, 0, 2))

        xp, xp_bf, g_tail = _ffn(gp, xp, xp_bf, i, ffn_hist_p, w_ffn_gate, w_ffn_up, w_ffn_conv,
                                 b_ffn_conv, w_ffn_down, ln_ffn_g, ln_ffn_b, alpha, tf)
        outs["ffnp"].append(g_tail.reshape(B, SUBLANES, F)[:, SUBLANES - 2:])
        xs, xs_bf, g_tail = _ffn(gs, xs, xs_bf, i, ffn_hist_s[i], w_ffn_gate, w_ffn_up, w_ffn_conv,
                                 b_ffn_conv, w_ffn_down, ln_ffn_g, ln_ffn_b, alpha, tf)
        outs["ffns"].append(g_tail.reshape(2, Bd, F).transpose(1, 0, 2))

    st = lambda name: jnp.stack(outs[name])
    y_prompt = xp.reshape(B, T, D)
    y_sample = xs.reshape(Tn, Bd, D).transpose(1, 0, 2)
    return (y_prompt, y_sample, st("kp"), st("vp"), st("kip"), st("convp"), st("ffnp"),
            st("ks"), st("vs"), st("kis"), st("convs"), st("ffns"))
```

```python
import functools

import jax
import jax.numpy as jnp
from jax import lax
from jax.experimental import pallas as pl
from jax.experimental.pallas import tpu as pltpu

F32 = jnp.float32
BF16 = jnp.bfloat16
I32 = jnp.int32

N_HEADS = 16
N_KV_HEADS = 4
HEAD_DIM = 128
GROUP = N_HEADS // N_KV_HEADS
ROT_DIM = HEAD_DIM // 4
ROPE_THETA = 500000.0
N_IDX_HEADS = 16
IDX_DIM = 64
IDX_ROT_DIM = IDX_DIM // 4
IDX_W_SCALE = (N_IDX_HEADS * IDX_DIM) ** -0.5
TOPK_MAX = 256
Q_BLOCK = 128
LN_EPS = 1e-5
LANES = 128
SUBLANES = 8
PAGES_PER_STEP = 8
T_PAD = 8
ONES_ROWS = 16
NEG_BIG = -1e30
INT_MIN = -(2 ** 31)
INT_MAX = 2 ** 31 - 1
NEG_INF_KEY = INT_MIN + 0x007FFFFF
VMEM_LIMIT = 56 * 2 ** 20


def _params(*sem):
    return pltpu.CompilerParams(dimension_semantics=sem, vmem_limit_bytes=VMEM_LIMIT)


def _dot(a, b):
    return jnp.dot(a, b, preferred_element_type=F32)


def _dot_nt(a, b):
    return lax.dot_general(a, b, (((1,), (1,)), ((), ())), preferred_element_type=F32)


def _layer_norm(z, g, b):
    mu = jnp.mean(z, axis=-1, keepdims=True)
    zc = z - mu
    var = jnp.mean(zc * zc, axis=-1, keepdims=True)
    return zc * lax.rsqrt(var + LN_EPS) * g + b


def _lane_tile(x, rep):
    return x if rep == 1 else jnp.concatenate([x] * rep, axis=1)


def _proj_body(*refs, shift, n_valid):
    x_ref, w_ref = refs[0], refs[1]
    if shift:
        c_ref, s1_ref, s2_ref = refs[2:5]
        outs = refs[5:]
    else:
        outs = refs[2:]
    x = x_ref[...]
    tn = w_ref.shape[0]
    sub = min(tn, 256)
    rep = sub // LANES
    for c0 in range(0, tn, sub):
        y = _dot_nt(x, w_ref[c0:c0 + sub, :].astype(BF16))
        if n_valid < tn:
            col = c0 + lax.broadcasted_iota(I32, y.shape, 1)
            y = jnp.where(col < n_valid, y, 0.0)
        if shift:
            y = (y * _lane_tile(c_ref[...], rep)
                 + pltpu.roll(y, shift, 1) * _lane_tile(s1_ref[...], rep)
                 + pltpu.roll(y, sub - shift, 1) * _lane_tile(s2_ref[...], rep))
        for o in outs:
            o[:, c0:c0 + sub] = y.astype(o.dtype)


def _proj(x_bf, wt, layer, col_off, ncols, tabs, shift, out_dtypes, tm, tn):
    M, K = x_bf.shape
    assert col_off % tn == 0 and ncols % tn == 0 and M % tm == 0
    cb = col_off // tn
    n_valid = min(tn, wt.shape[1] - col_off - (ncols - tn))
    in_specs = [pl.BlockSpec((tm, K), lambda i, j: (i, 0)),
                pl.BlockSpec((None, tn, K), lambda i, j: (layer, cb + j, 0))]
    args = [x_bf, wt]
    if shift:
        period = tabs[0].shape[0] // tm
        for t in tabs:
            in_specs.append(pl.BlockSpec((tm, LANES), lambda i, j: (i % period, 0)))
            args.append(t)
    outs = pl.pallas_call(
        functools.partial(_proj_body, shift=shift, n_valid=n_valid),
        grid=(M // tm, ncols // tn),
        in_specs=in_specs,
        out_specs=[pl.BlockSpec((tm, tn), lambda i, j: (i, j)) for _ in out_dtypes],
        out_shape=[jax.ShapeDtypeStruct((M, ncols), dt) for dt in out_dtypes],
        compiler_params=_params("parallel", "arbitrary"),
        name="proj",
    )(*args)
    return outs


def _glu_body(x_ref, wa_ref, wg_ref, ba_ref, bg_ref, a_ref):
    x = x_ref[...]
    tn = wa_ref.shape[1]
    sub = min(tn, 256)
    for c0 in range(0, tn, sub):
        cols = slice(c0, c0 + sub)
        a = _dot(x, wa_ref[:, cols].astype(BF16)) + ba_ref[:, cols]
        g = _dot(x, wg_ref[:, cols].astype(BF16)) + bg_ref[:, cols]
        a_ref[:, cols] = a * jax.nn.sigmoid(g)


def _glu(x_bf, w_pw1, b_pw1, layer, tm, tn):
    M, K = x_bf.shape
    C = w_pw1.shape[2] // 2
    nb = C // tn
    b2 = b_pw1[layer].reshape(1, 2 * C)
    return pl.pallas_call(
        _glu_body,
        grid=(M // tm, nb),
        in_specs=[pl.BlockSpec((tm, K), lambda i, j: (i, 0)),
                  pl.BlockSpec((None, K, tn), lambda i, j: (layer, 0, j)),
                  pl.BlockSpec((None, K, tn), lambda i, j: (layer, 0, nb + j)),
                  pl.BlockSpec((1, tn), lambda i, j: (0, j)),
                  pl.BlockSpec((1, tn), lambda i, j: (0, nb + j))],
        out_specs=pl.BlockSpec((tm, tn), lambda i, j: (i, j)),
        out_shape=jax.ShapeDtypeStruct((M, C), F32),
        compiler_params=_params("parallel", "arbitrary"),
        name="glu",
    )(x_bf, w_pw1, w_pw1, b2, b2)


def _dwconv_body(a_ref, hist_ref, w_ref, b_ref, g_ref, beta_ref, h_ref, ext_ref, conv_ref, sh_ref,
                 *, stride, hist_rows, width, seq_tiles, row_block):
    tm, C = a_ref.shape
    R = hist_rows + tm
    hist = hist_ref[...]
    if seq_tiles:
        hist = jnp.where(pl.program_id(0) % seq_tiles == 0, 0.0, hist)
    ext_ref[0:hist_rows, :] = hist
    ext_ref[hist_rows:R, :] = a_ref[...]
    tap_off = [hist_rows - (width - 1 - k) * stride for k in range(width)]
    residues = sorted({off % SUBLANES for off in tap_off} - {0})

    def chunk(ci, carry):
        c0 = pl.multiple_of(ci * LANES, LANES)
        wch = w_ref[:, pl.ds(c0, LANES)]
        bch = b_ref[:, pl.ds(c0, LANES)]
        for r in residues:
            sh_ref[r, 0:R - SUBLANES, :] = ext_ref[pl.ds(r, R - SUBLANES), pl.ds(c0, LANES)]
        for r0 in range(0, tm, row_block):
            acc = jnp.broadcast_to(bch, (row_block, LANES))
            for k in range(width):
                r = tap_off[k] % SUBLANES
                base = tap_off[k] - r + r0
                if r == 0:
                    tap = ext_ref[pl.ds(base, row_block), pl.ds(c0, LANES)]
                else:
                    tap = sh_ref[r, pl.ds(base, row_block), :]
                acc = acc + tap * wch[k:k + 1, :]
            conv_ref[pl.ds(r0, row_block), pl.ds(c0, LANES)] = acc
        return carry

    lax.fori_loop(0, C // LANES, chunk, 0)
    y = _layer_norm(conv_ref[...], g_ref[...], beta_ref[...])
    h_ref[...] = (y * jax.nn.sigmoid(y)).astype(h_ref.dtype)


def _dwconv(a, hist, w_dw, b_dw, ln_g, ln_b, *, tm, stride, hist_rows, seq_tiles):
    M, C = a.shape
    width = w_dw.shape[0]
    if seq_tiles:
        per = tm // hist_rows
        hist_arr = a
        hist_spec = pl.BlockSpec((hist_rows, C), lambda i: (jnp.maximum(i * per - 1, 0), 0))
    else:
        hist_arr = hist
        hist_spec = pl.BlockSpec((hist_rows, C), lambda i: (0, 0))
    vec = lambda: pl.BlockSpec((1, C), lambda i: (0, 0))
    return pl.pallas_call(
        functools.partial(_dwconv_body, stride=stride, hist_rows=hist_rows, width=width,
                          seq_tiles=seq_tiles, row_block=min(tm, 64)),
        grid=(M // tm,),
        in_specs=[pl.BlockSpec((tm, C), lambda i: (i, 0)), hist_spec,
                  pl.BlockSpec((width, C), lambda i: (0, 0)), vec(), vec(), vec()],
        out_specs=pl.BlockSpec((tm, C), lambda i: (i, 0)),
        out_shape=jax.ShapeDtypeStruct((M, C), BF16),
        scratch_shapes=[pltpu.VMEM((hist_rows + tm, C), F32), pltpu.VMEM((tm, C), F32),
                        pltpu.VMEM((SUBLANES, hist_rows + tm, LANES), F32)],
        compiler_params=_params("arbitrary"),
        name="dwconv",
    )(a, hist_arr, w_dw, b_dw.reshape(1, C), ln_g.reshape(1, C), ln_b.reshape(1, C))


def _mm_res_ln_body(a_ref, w_ref, bias_ref, res_ref, g_ref, b_ref, xo_ref, xb_ref, *, nk, alpha):
    k = pl.program_id(1)

    @pl.when(k == 0)
    def _():
        xo_ref[...] = jnp.zeros(xo_ref.shape, F32)

    a = a_ref[...]
    N = xo_ref.shape[1]
    nc = min(N, 512)
    for c0 in range(0, N, nc):
        xo_ref[:, c0:c0 + nc] += _dot(a, w_ref[:, c0:c0 + nc].astype(BF16))

    @pl.when(k == nk - 1)
    def _():
        z = alpha * res_ref[...] + (xo_ref[...] + bias_ref[...])
        out = _layer_norm(z, g_ref[...], b_ref[...])
        xo_ref[...] = out
        xb_ref[...] = out.astype(BF16)


def _mm_res_ln(a_bf, w, layer, bias, res, ln_g, ln_b, alpha, tm, tk):
    M, K = a_bf.shape
    N = w.shape[2]
    assert K % tk == 0
    nk = K // tk
    vec = lambda: pl.BlockSpec((1, N), lambda i, k: (0, 0))
    return pl.pallas_call(
        functools.partial(_mm_res_ln_body, nk=nk, alpha=alpha),
        grid=(M // tm, nk),
        in_specs=[pl.BlockSpec((tm, tk), lambda i, k: (i, k)),
                  pl.BlockSpec((None, tk, N), lambda i, k: (layer, k, 0)),
                  vec(),
                  pl.BlockSpec((tm, N), lambda i, k: (i, 0), pipeline_mode=pl.Buffered(1)),
                  vec(), vec()],
        out_specs=[pl.BlockSpec((tm, N), lambda i, j: (i, 0)),
                   pl.BlockSpec((tm, N), lambda i, j: (i, 0))],
        out_shape=[jax.ShapeDtypeStruct((M, N), F32), jax.ShapeDtypeStruct((M, N), BF16)],
        compiler_params=_params("parallel", "arbitrary"),
        name="mm_res_ln",
    )(a_bf, w, bias.reshape(1, N), res, ln_g.reshape(1, N), ln_b.reshape(1, N))


def _ffn_up_body(x_ref, wg_ref, wu_ref, hist_ref, wc_ref, bc_ref, h_ref, gt_ref, ext_ref,
                 *, stride, hist_rows, sub):
    x = x_ref[...]
    tm = x.shape[0]
    H = hist_rows
    tf = wg_ref.shape[1]
    for c0 in range(0, tf, sub):
        cols = slice(c0, c0 + sub)
        g = _dot(x, wg_ref[:, cols].astype(BF16))
        ext_ref[0:H, cols] = hist_ref[:, cols]
        ext_ref[H:H + tm, cols] = g
        gt_ref[:, cols] = g[tm - H:tm, :]
        u = _dot(x, wu_ref[:, cols].astype(BF16))
        wc = wc_ref[:, cols]
        c = (bc_ref[:, cols] + wc[2:3, :] * g
             + wc[1:2, :] * ext_ref[H - stride:H - stride + tm, cols]
             + wc[0:1, :] * ext_ref[H - 2 * stride:H - 2 * stride + tm, cols])
        h_ref[:, cols] = (c * jax.nn.sigmoid(c) * u).astype(h_ref.dtype)


def _ffn_up(x_bf, w_gate, w_up, layer, hist, w_c, b_c, *, tm, tf, stride, hist_rows):
    M, K = x_bf.shape
    F = w_gate.shape[2]
    H = hist_rows
    return pl.pallas_call(
        functools.partial(_ffn_up_body, stride=stride, hist_rows=H, sub=min(tf, 256)),
        grid=(M // tm, F // tf),
        in_specs=[pl.BlockSpec((tm, K), lambda i, j: (i, 0)),
                  pl.BlockSpec((None, K, tf), lambda i, j: (layer, 0, j)),
                  pl.BlockSpec((None, K, tf), lambda i, j: (layer, 0, j)),
                  pl.BlockSpec((H, tf), lambda i, j: (i, j)),
                  pl.BlockSpec((3, tf), lambda i, j: (0, j)),
                  pl.BlockSpec((1, tf), lambda i, j: (0, j))],
        out_specs=[pl.BlockSpec((tm, tf), lambda i, j: (i, j)),
                   pl.BlockSpec((H, tf), lambda i, j: (i, j))],
        out_shape=[jax.ShapeDtypeStruct((M, F), BF16),
                   jax.ShapeDtypeStruct((M // tm * H, F), F32)],
        scratch_shapes=[pltpu.VMEM((H + tm, tf), F32)],
        compiler_params=_params("parallel", "arbitrary"),
        name="ffn_up",
    )(x_bf, w_gate, w_up, hist, w_c, b_c.reshape(1, F))


def _order_key(score):
    bits = pltpu.bitcast(score + 0.0, I32)
    return jnp.where(bits < 0, bits ^ INT_MAX, bits)


COUNT_CHAINS = 8


def _count(mask):
    S, L = mask.shape
    assert S < 2 ** 24
    x = jnp.where(mask, 1.0, 0.0).astype(F32)
    group = COUNT_CHAINS * SUBLANES
    if S % group == 0 and S > group:
        chains = [x[c * SUBLANES:(c + 1) * SUBLANES] for c in range(COUNT_CHAINS)]
        for r in range(group, S, group):
            for c in range(COUNT_CHAINS):
                chains[c] = chains[c] + x[r + c * SUBLANES:r + (c + 1) * SUBLANES]
        while len(chains) > 1:
            chains = [chains[i] + chains[i + 1] for i in range(0, len(chains), 2)]
        x = chains[0]
    return jnp.sum(x, axis=0, keepdims=True).astype(I32)


def _select_topk(key_ref, cut_ref, k, lane_ok):
    S, L = key_ref.shape
    cnt0 = _count(key_ref[...] >= 0)
    base = jnp.where(cnt0 >= k, 0, INT_MIN).astype(I32)

    def bit_step(t, base):
        cand = base | lax.shift_left(jnp.int32(1), 30 - t)
        return jnp.where(_count(key_ref[...] >= cand) >= k, cand, base)

    thr = lax.fori_loop(0, 31, bit_step, base)
    key = key_ref[...]
    n_gt = _count(key > thr)
    n_ge = _count(key >= thr)
    need = k - n_gt
    tie = (n_ge > k) & (thr > NEG_INF_KEY) & lane_ok
    cut_ref[...] = jnp.full((1, L), INT_MAX, I32)
    n_tie = jnp.sum(jnp.where(tie, 1, 0).astype(I32))

    @pl.when(n_tie > 0)
    def _():
        nbits = max(1, (S - 1).bit_length())
        rows = lax.broadcasted_iota(I32, (S, L), 0)

        def idx_step(t, pos):
            cand = pos | lax.shift_left(jnp.int32(1), nbits - 1 - t)
            c = _count((key_ref[...] == thr) & (rows < cand))
            return jnp.where(c < need, cand, pos)

        cut_ref[...] = lax.fori_loop(0, nbits, idx_step, jnp.zeros((1, L), I32))

    return thr


def _attn_prompt_body(qi_ref, wit_ref, kiwi_ref, q_ref, k_ref, v_ref, o_ref,
                      ki2_ref, sc_ref, key_ref, cut_ref, vt_ref, *, k_sel, n_var):
    S = kiwi_ref.shape[0]
    tq = qi_ref.shape[0]
    qb = pl.program_id(1)

    @pl.when(qb == 0)
    def _():
        lane = lax.broadcasted_iota(I32, (S, LANES), 1)
        lo = jnp.where(lane < IDX_DIM, kiwi_ref[...], 0.0)
        ki2_ref[0:S, :] = lo.astype(BF16)
        ki2_ref[S:2 * S, :] = pltpu.roll(lo, IDX_DIM, 1).astype(BF16)
        for n in range(N_KV_HEADS):
            vt_ref[n, 0:HEAD_DIM, :] = v_ref[:, n * HEAD_DIM:(n + 1) * HEAD_DIM].T.astype(BF16)
            vt_ref[n, HEAD_DIM:HEAD_DIM + ONES_ROWS, :] = jnp.ones((ONES_ROWS, S), BF16)

    ws = wit_ref[...] * IDX_W_SCALE
    scale = HEAD_DIM ** -0.5

    def attend(su):
        sc = sc_ref.at[0:su]
        keys = key_ref.at[0:su]
        for p in range(N_IDX_HEADS // 2):
            qp = qi_ref[:, p * LANES:(p + 1) * LANES]
            d0 = _dot_nt(ki2_ref[0:su, :], qp)
            d1 = _dot_nt(ki2_ref[S:S + su, :], qp)
            part = (ws[2 * p:2 * p + 1, :] * jnp.maximum(d0, 0.0)
                    + ws[2 * p + 1:2 * p + 2, :] * jnp.maximum(d1, 0.0))
            if p == 0:
                sc[...] = part
            else:
                sc[...] += part

        rows = lax.broadcasted_iota(I32, (su, tq), 0)
        qpos = qb * tq + lax.broadcasted_iota(I32, (su, tq), 1)
        causal = rows <= qpos
        keys[...] = _order_key(jnp.where(causal, sc[...], -jnp.inf))
        thr = _select_topk(keys, cut_ref, k_sel, jnp.full((1, tq), True))
        key = keys[...]
        sel = ((key > thr) | ((key == thr) & (rows <= cut_ref[...]))) & causal
        bias = _lane_tile(jnp.where(sel, 0.0, NEG_BIG), GROUP)

        for n in range(N_KV_HEADS):
            qs = jnp.concatenate([q_ref[:, (n * GROUP + g) * HEAD_DIM:(n * GROUP + g + 1) * HEAD_DIM]
                                  for g in range(GROUP)], axis=0)
            s = _dot_nt(k_ref[0:su, n * HEAD_DIM:(n + 1) * HEAD_DIM], qs) + bias
            m = jnp.max(s, axis=0, keepdims=True)
            e = jnp.exp((s - m) * scale).astype(BF16)
            ov = _dot(vt_ref[n, :, 0:su], e)
            ot = ov[0:HEAD_DIM, :] * (1.0 / ov[HEAD_DIM:HEAD_DIM + 1, :])
            for g in range(GROUP):
                h = n * GROUP + g
                o_ref[:, h * HEAD_DIM:(h + 1) * HEAD_DIM] = (
                    ot[:, g * tq:(g + 1) * tq].T.astype(o_ref.dtype))

    nb = S // tq
    per = nb // n_var
    for c in range(n_var):
        pl.when(qb // per == c)(functools.partial(attend, (c + 1) * per * tq))


def _attn_prompt(qi_bf, wit, kiwi, q_bf, k_bf, v_f, B, T, k_sel):
    tq = Q_BLOCK
    nb = T // tq
    n_var = 4 if nb % 4 == 0 else 1
    QC = N_HEADS * HEAD_DIM
    KC = N_KV_HEADS * HEAD_DIM
    QI = N_IDX_HEADS * IDX_DIM
    return pl.pallas_call(
        functools.partial(_attn_prompt_body, k_sel=k_sel, n_var=n_var),
        grid=(B, nb),
        in_specs=[pl.BlockSpec((tq, QI), lambda b, i: (b * nb + i, 0)),
                  pl.BlockSpec((N_IDX_HEADS, tq), lambda b, i: (0, b * nb + i)),
                  pl.BlockSpec((T, LANES), lambda b, i: (b, 0)),
                  pl.BlockSpec((tq, QC), lambda b, i: (b * nb + i, 0)),
                  pl.BlockSpec((T, KC), lambda b, i: (b, 0)),
                  pl.BlockSpec((T, KC), lambda b, i: (b, 0))],
        out_specs=pl.BlockSpec((tq, QC), lambda b, i: (b * nb + i, 0)),
        out_shape=jax.ShapeDtypeStruct((B * T, QC), BF16),
        scratch_shapes=[pltpu.VMEM((2 * T, LANES), BF16), pltpu.VMEM((T, tq), F32),
                        pltpu.VMEM((T, tq), I32), pltpu.VMEM((1, tq), I32),
                        pltpu.VMEM((N_KV_HEADS, HEAD_DIM + ONES_ROWS, T), BF16)],
        compiler_params=_params("parallel", "arbitrary"),
        name="attn_prompt",
    )(qi_bf, wit, kiwi, q_bf, k_bf, v_f)


def _page_scores(kit, qi, w_col):
    d = _dot(qi, kit.astype(BF16))
    z = jnp.maximum(d, 0.0) * w_col
    return z.reshape(N_IDX_HEADS, T_PAD, LANES).sum(axis=0)


def _samp_score_body(pt_ref, *refs, n_tok):
    G = PAGES_PER_STEP
    pages = refs[:G]
    kinew_ref, qi_ref, wc_ref, out_ref = refs[G:G + 4]
    p = pl.program_id(1)
    n_steps = pl.num_programs(1)
    qi = qi_ref[...]
    wc = wc_ref[...]

    @pl.when(p < n_steps - 1)
    def _():
        for g in range(G):
            out_ref[:, g * LANES:(g + 1) * LANES] = _page_scores(pages[g][...], qi, wc)

    @pl.when(p == n_steps - 1)
    def _():
        z = _page_scores(kinew_ref[...], qi, wc)
        tok = lax.broadcasted_iota(I32, (T_PAD, LANES), 0)
        new = lax.broadcasted_iota(I32, (T_PAD, LANES), 1)
        ok = (new <= tok) & (new < n_tok)
        out_ref[:, 0:LANES] = jnp.where(ok, z, -jnp.inf)
        out_ref[:, LANES:G * LANES] = jnp.full((T_PAD, (G - 1) * LANES), -jnp.inf, F32)


def _page_index_map(g, layer_off, n_steps, rank=3):
    def index_map(b, p, pt):
        pg = jnp.minimum(p, n_steps - 2) * PAGES_PER_STEP + g % PAGES_PER_STEP
        return (layer_off + pt[b, pg],) + (0,) * (rank - 1)
    return index_map


def _samp_score(page_table, cache_kit, layer_off, kitnew, qi_rows, w_col, n_tok):
    Bd, n_pages = page_table.shape
    G = PAGES_PER_STEP
    n_steps = n_pages // G + 1
    page = cache_kit.shape[2]
    in_specs = [pl.BlockSpec((None, IDX_DIM, page), _page_index_map(g, layer_off, n_steps))
                for g in range(G)]
    in_specs += [pl.BlockSpec((None, IDX_DIM, LANES), lambda b, p, pt: (b, 0, 0)),
                 pl.BlockSpec((None, LANES, IDX_DIM), lambda b, p, pt: (b, 0, 0)),
                 pl.BlockSpec((None, LANES, 1), lambda b, p, pt: (b, 0, 0))]
    return pl.pallas_call(
        functools.partial(_samp_score_body, n_tok=n_tok),
        grid_spec=pltpu.PrefetchScalarGridSpec(
            num_scalar_prefetch=1, grid=(Bd, n_steps), in_specs=in_specs,
            out_specs=pl.BlockSpec((None, T_PAD, G * page), lambda b, p, pt: (b, 0, p))),
        out_shape=jax.ShapeDtypeStruct((Bd, T_PAD, n_steps * G * page), F32),
        compiler_params=_params("parallel", "arbitrary"),
        name="samp_score",
    )(page_table, *([cache_kit] * G), kitnew, qi_rows, w_col)


def _samp_select_body(sc_ref, sel_ref, key_ref, cut_ref, *, k_sel, n_lanes, n_tok):
    S = sc_ref.shape[0]
    key_ref[...] = _order_key(sc_ref[...])
    lane = lax.broadcasted_iota(I32, (1, LANES), 1)
    lane_ok = (lane < n_lanes) & (lane % T_PAD < n_tok)
    thr = _select_topk(key_ref, cut_ref, k_sel, lane_ok)
    key = key_ref[...]
    rows = lax.broadcasted_iota(I32, (S, LANES), 0)
    sel = (key > thr) | ((key == thr) & (rows <= cut_ref[...]))
    sel_ref[...] = jnp.where(sel, 1.0, 0.0).astype(sel_ref.dtype)


def _samp_select(score_t, k_sel, n_lanes, n_tok):
    S = score_t.shape[0]
    return pl.pallas_call(
        functools.partial(_samp_select_body, k_sel=k_sel, n_lanes=n_lanes, n_tok=n_tok),
        out_shape=jax.ShapeDtypeStruct((S, LANES), BF16),
        scratch_shapes=[pltpu.VMEM((S, LANES), I32), pltpu.VMEM((1, LANES), I32)],
        compiler_params=pltpu.CompilerParams(vmem_limit_bytes=VMEM_LIMIT),
        name="samp_select",
    )(score_t)


def _samp_attend_body(pt_ref, *refs):
    G = PAGES_PER_STEP
    kp = refs[:G]
    vp = refs[G:2 * G]
    (knew_ref, vnew_ref, sel_ref, pick_ref, qcat_ref, o_ref,
     m_ref, l_ref, acc_ref) = refs[2 * G:]
    p = pl.program_id(1)
    n_steps = pl.num_programs(1)
    scale = HEAD_DIM ** -0.5
    R = knew_ref.shape[0]

    @pl.when(p == 0)
    def _():
        m_ref[...] = jnp.full(m_ref.shape, NEG_BIG, F32)
        l_ref[...] = jnp.zeros(l_ref.shape, F32)
        acc_ref[...] = jnp.zeros(acc_ref.shape, F32)

    qcat = qcat_ref[...]
    pick = pick_ref[...]
    rr = lax.broadcasted_iota(I32, (R, LANES), 0)
    cc = lax.broadcasted_iota(I32, (R, LANES), 1)
    head_bias = jnp.where(rr % N_KV_HEADS == cc // (GROUP * T_PAD), 0.0, NEG_BIG)
    rep = jnp.where(rr // N_KV_HEADS == cc, 1.0, 0.0).astype(BF16)

    chosen = _dot(sel_ref[...], pick)
    bias_all = ((chosen - 1.0) * (-NEG_BIG / scale)).astype(BF16)

    def page_logits(k_ref, g):
        lhs = jnp.concatenate([k_ref[...].astype(BF16), rep], axis=1)
        rhs = jnp.concatenate([qcat, bias_all[g * LANES:(g + 1) * LANES, :]], axis=0)
        return _dot(lhs, rhs) * scale + head_bias

    def fold(k_refs, v_refs):
        logits = [page_logits(k_ref, g) for g, k_ref in enumerate(k_refs)]
        m_old = m_ref[...]
        m_new = m_old
        for s in logits:
            m_new = jnp.maximum(m_new, jnp.max(s, axis=0, keepdims=True))
        corr = jnp.exp(m_old - m_new)
        l_new = l_ref[...] * corr
        acc = acc_ref[...] * corr
        for s, v_ref in zip(logits, v_refs):
            e = jnp.exp(s - m_new)
            l_new = l_new + jnp.sum(e, axis=0, keepdims=True)
            acc = acc + _dot(v_ref[...].T.astype(BF16), e.astype(BF16))
        m_ref[...] = m_new
        l_ref[...] = l_new
        acc_ref[...] = acc

    @pl.when(p < n_steps - 1)
    def _():
        fold(kp, vp)

    @pl.when(p == n_steps - 1)
    def _():
        fold([knew_ref], [vnew_ref])
        o_ref[...] = acc_ref[...] * (1.0 / l_ref[...])


def _samp_attend(page_table, cache_k, cache_v, layer_off, knew, vnew, sel_mask, pick, qcat):
    Bd, n_pages = page_table.shape
    G = PAGES_PER_STEP
    n_steps = n_pages // G + 1
    R = cache_k.shape[1]
    page = R // N_KV_HEADS
    in_specs = [pl.BlockSpec((None, R, HEAD_DIM), _page_index_map(g, layer_off, n_steps))
                for g in range(2 * G)]
    per_seq = lambda r, c: pl.BlockSpec((None, r, c), lambda b, p, pt: (b, 0, 0))
    in_specs += [per_seq(R, HEAD_DIM), per_seq(R, HEAD_DIM),
                 pl.BlockSpec((G * page, LANES), lambda b, p, pt: (p, 0)),
                 per_seq(LANES, LANES), per_seq(HEAD_DIM, LANES)]
    return pl.pallas_call(
        _samp_attend_body,
        grid_spec=pltpu.PrefetchScalarGridSpec(
            num_scalar_prefetch=1, grid=(Bd, n_steps), in_specs=in_specs,
            out_specs=pl.BlockSpec((None, HEAD_DIM, LANES), lambda b, p, pt: (b, 0, 0)),
            scratch_shapes=[pltpu.VMEM((1, LANES), F32), pltpu.VMEM((1, LANES), F32),
                            pltpu.VMEM((HEAD_DIM, LANES), F32)]),
        out_shape=jax.ShapeDtypeStruct((Bd, HEAD_DIM, LANES), F32),
        compiler_params=_params("parallel", "arbitrary"),
        name="samp_attend",
    )(page_table, *([cache_k] * G), *([cache_v] * G), knew, vnew, sel_mask, pick, qcat)


def _rope_tables(pos, rot_dim, period):
    half = rot_dim // 2
    inv = ROPE_THETA ** (-jnp.arange(0, rot_dim, 2, dtype=F32) / rot_dim)
    ang = pos.astype(F32)[:, None] * inv[None, :]
    cos, sin = jnp.cos(ang), jnp.sin(ang)
    lane = jnp.arange(LANES) % period
    fi = lane % half
    c = jnp.where(lane[None, :] < rot_dim, cos[:, fi], 1.0)
    s_hi = jnp.where((lane[None, :] >= half) & (lane[None, :] < rot_dim), sin[:, fi], 0.0)
    s_lo = jnp.where(lane[None, :] < half, -sin[:, fi], 0.0)
    return c, s_hi, s_lo


class _Group:
    def __init__(self, rows, seq_rows, stride, pos, tm, tm_down):
        self.rows = rows
        self.seq_rows = seq_rows
        self.stride = stride
        self.tm = tm
        self.tm_down = tm_down
        self.tab_q = _rope_tables(pos, ROT_DIM, HEAD_DIM)
        self.tab_i = _rope_tables(pos, IDX_ROT_DIM, IDX_DIM)


def _in_proj(grp, x_bf, wt_in, j):
    QC = N_HEADS * HEAD_DIM
    KC = N_KV_HEADS * HEAD_DIM
    QI = N_IDX_HEADS * IDX_DIM
    tm = grp.tm
    (q_bf,) = _proj(x_bf, wt_in, j, 0, QC, grp.tab_q, ROT_DIM // 2, [BF16], tm, 512)
    k_f, k_bf = _proj(x_bf, wt_in, j, QC, KC, grp.tab_q, ROT_DIM // 2, [F32, BF16], tm, 512)
    (v_f,) = _proj(x_bf, wt_in, j, QC + KC, KC, None, 0, [F32], tm, 512)
    (qi_bf,) = _proj(x_bf, wt_in, j, QC + 2 * KC, QI, grp.tab_i, IDX_ROT_DIM // 2, [BF16], tm, 512)
    c, s_hi, s_lo = grp.tab_i
    lane = jnp.arange(LANES)[None, :]
    keep = lane < IDX_DIM
    tab_kw = (jnp.where(keep, c, 1.0), jnp.where(keep, s_hi, 0.0), jnp.where(keep, s_lo, 0.0))
    (kiwi,) = _proj(x_bf, wt_in, j, QC + 2 * KC + QI, LANES, tab_kw, IDX_ROT_DIM // 2, [F32], tm, LANES)
    return q_bf, k_f, k_bf, v_f, qi_bf, kiwi


def _ffn(grp, x, x_bf, i, hist, w_gate, w_up, w_c, b_c, w_down, ln_g, ln_b, alpha, tf):
    F = w_gate.shape[2]
    H = hist.shape[0] // (grp.rows // grp.seq_rows)
    h_bf, g_tail = _ffn_up(x_bf, w_gate, w_up, i, hist, w_c[i], b_c[i],
                           tm=grp.seq_rows, tf=tf, stride=grp.stride, hist_rows=H)
    x, x_bf = _mm_res_ln(h_bf, w_down, i, jnp.zeros((w_down.shape[2],), F32), x, ln_g[i], ln_b[i],
                         alpha, grp.tm_down, 512)
    return x, x_bf, g_tail


def kernel(x_prompt, x_sample, cache_k, cache_v, cache_kidx, state_conv, state_ffn, page_table,
           w_attn_in, w_attn_out, w_pw1, b_pw1, w_dw, b_dw, ln_conv_g, ln_conv_b, w_pw2, b_pw2,
           w_ffn_gate, w_ffn_up, w_ffn_conv, b_ffn_conv, w_ffn_down,
           ln_mix_g, ln_mix_b, ln_ffn_g, ln_ffn_b):
    B, T, D = x_prompt.shape
    Bd, Tn, _ = x_sample.shape
    depth = w_ffn_gate.shape[0]
    F = w_ffn_gate.shape[2]
    n_attn, n_pool, page = cache_k.shape[:3]
    n_conv = w_pw1.shape[0]
    cw = w_dw.shape[1]
    past = page_table.shape[1] * page
    KC = N_KV_HEADS * HEAD_DIM
    alpha = float((2 * depth) ** 0.25)
    assert Bd == SUBLANES and Tn <= T_PAD and D == N_HEADS * HEAD_DIM
    tf = 512 if F % 512 == 0 else LANES

    gp = _Group(B * T, T, 1, jnp.arange(T, dtype=I32), min(1024, T), min(1024, T))
    pos_s = past + jnp.arange(Tn, dtype=I32)
    gs = _Group(Tn * Bd, Tn * Bd, Bd, jnp.repeat(pos_s, Bd), Tn * Bd, Tn * Bd)

    xp = x_prompt.reshape(B * T, D)
    xs = x_sample.transpose(1, 0, 2).reshape(Tn * Bd, D)
    xp_bf, xs_bf = xp.astype(BF16), xs.astype(BF16)

    ck = cache_k.reshape(n_attn * n_pool, page * N_KV_HEADS, HEAD_DIM)
    cv = cache_v.reshape(n_attn * n_pool, page * N_KV_HEADS, HEAD_DIM)
    ckit = cache_kidx.reshape(n_attn * n_pool, page, IDX_DIM).transpose(0, 2, 1)
    conv_hist_s = state_conv.transpose(0, 2, 1, 3).reshape(n_conv, (cw - 1) * Bd, D)
    ffn_hist_s = state_ffn.transpose(0, 2, 1, 3).reshape(depth, 2 * Bd, F)
    ffn_hist_p = jnp.zeros((B * SUBLANES, F), F32)
    zero_d = jnp.zeros((D,), F32)

    wt_in = w_attn_in.transpose(0, 2, 1)
    w_attn_out = w_attn_out.astype(BF16)
    w_pw2 = w_pw2.astype(BF16)
    w_ffn_down = w_ffn_down.astype(BF16)

    src = jnp.arange(LANES)
    pick = ((src[None, :, None] // T_PAD == jnp.arange(Bd)[:, None, None])
            & (src[None, :, None] % T_PAD == src[None, None, :] % T_PAD)).astype(BF16)

    k_sel_p = min(TOPK_MAX, T // 4)
    k_sel_s = min(TOPK_MAX, (past + Tn) // 4)

    outs = {name: [] for name in ("kp", "vp", "kip", "convp", "ffnp", "ks", "vs", "kis", "convs", "ffns")}

    for i in range(depth):
        j = i // 2
        if i % 2 == 0:
            q_bf, k_f, k_bf, v_f, qi_bf, kiwi = _in_proj(gp, xp_bf, wt_in, j)
            wit = kiwi[:, IDX_DIM:IDX_DIM + N_IDX_HEADS].T
            o_bf = _attn_prompt(qi_bf, wit, kiwi, q_bf, k_bf, v_f, B, T, k_sel_p)
            xp, xp_bf = _mm_res_ln(o_bf, w_attn_out, j, zero_d, xp, ln_mix_g[i], ln_mix_b[i],
                                   alpha, gp.tm_down, 1024)
            outs["kp"].append(k_f.reshape(B, T, N_KV_HEADS, HEAD_DIM))
            outs["vp"].append(v_f.reshape(B, T, N_KV_HEADS, HEAD_DIM))
            outs["kip"].append(kiwi[:, :IDX_DIM].reshape(B, T, IDX_DIM))

            q_bf, k_f, k_bf, v_f, qi_bf, kiwi = _in_proj(gs, xs_bf, wt_in, j)
            tb = lambda a: a.reshape(Tn, Bd, -1).transpose(1, 0, 2)
            pad_rows = lambda a: jnp.pad(a, ((0, 0), (0, LANES - Tn), (0, 0)))
            kitnew = pad_rows(tb(kiwi[:, :IDX_DIM])).transpose(0, 2, 1)
            heads = lambda a: a.reshape(Bd, LANES * N_KV_HEADS, HEAD_DIM)
            knew, vnew = heads(pad_rows(tb(k_f))), heads(pad_rows(tb(v_f)))
            qi4 = tb(qi_bf).reshape(Bd, Tn, N_IDX_HEADS, IDX_DIM)
            qi4 = jnp.pad(qi4, ((0, 0), (0, T_PAD - Tn), (0, 0), (0, 0)))
            qi_rows = qi4.transpose(0, 2, 1, 3).reshape(Bd, N_IDX_HEADS * T_PAD, IDX_DIM)
            wi4 = tb(kiwi[:, IDX_DIM:IDX_DIM + N_IDX_HEADS]) * IDX_W_SCALE
            wi4 = jnp.pad(wi4, ((0, 0), (0, T_PAD - Tn), (0, 0)))
            w_col = wi4.transpose(0, 2, 1).reshape(Bd, N_IDX_HEADS * T_PAD, 1)
            score8 = _samp_score(page_table, ckit, j * n_pool, kitnew, qi_rows, w_col, Tn)
            s_tot = score8.shape[2]
            score_t = score8.transpose(2, 0, 1).reshape(s_tot, Bd * T_PAD)
            score_t = jnp.pad(score_t, ((0, 0), (0, LANES - Bd * T_PAD)))
            sel_mask = _samp_select(score_t, k_sel_s, Bd * T_PAD, Tn)
            q4 = tb(q_bf).reshape(Bd, Tn, N_KV_HEADS, GROUP, HEAD_DIM)
            q4 = jnp.pad(q4, ((0, 0), (0, T_PAD - Tn), (0, 0), (0, 0), (0, 0)))
            qcat = q4.transpose(0, 4, 2, 3, 1).reshape(Bd, HEAD_DIM, N_HEADS * T_PAD)
            ot = _samp_attend(page_table, ck, cv, j * n_pool, knew, vnew, sel_mask,
                              pick, qcat)
            o = ot.reshape(Bd, HEAD_DIM, N_HEADS, T_PAD)[..., :Tn]
            o = o.transpose(3, 0, 2, 1).reshape(Tn * Bd, N_HEADS * HEAD_DIM)
            xs, xs_bf = _mm_res_ln(o.astype(BF16), w_attn_out, j, zero_d, xs, ln_mix_g[i], ln_mix_b[i],
                                   alpha, gs.tm_down, 1024)
            outs["ks"].append(tb(k_f).reshape(Bd, Tn, N_KV_HEADS, HEAD_DIM))
            outs["vs"].append(tb(v_f).reshape(Bd, Tn, N_KV_HEADS, HEAD_DIM))
            outs["kis"].append(tb(kiwi[:, :IDX_DIM]))
        else:
            a = _glu(xp_bf, w_pw1, b_pw1, j, gp.tm, 512)
            h_bf = _dwconv(a, None, w_dw[j], b_dw[j], ln_conv_g[j], ln_conv_b[j],
                           tm=min(256, T), stride=1, hist_rows=32, seq_tiles=T // min(256, T))
            xp, xp_bf = _mm_res_ln(h_bf, w_pw2, j, b_pw2[j], xp, ln_mix_g[i], ln_mix_b[i],
                                   alpha, gp.tm_down, 1024)
            outs["convp"].append(a.reshape(B, T, D)[:, T - (cw - 1):])

            a = _glu(xs_bf, w_pw1, b_pw1, j, gs.tm, 512)
            h_bf = _dwconv(a, conv_hist_s[j], w_dw[j], b_dw[j], ln_conv_g[j], ln_conv_b[j],
                           tm=Tn * Bd, stride=Bd, hist_rows=(cw - 1) * Bd, seq_tiles=0)
            xs, xs_bf = _mm_res_ln(h_bf, w_pw2, j, b_pw2[j], xs, ln_mix_g[i], ln_mix_b[i],
                                   alpha, gs.tm_down, 1024)
            a_pad = jnp.concatenate([conv_hist_s[j], a], axis=0)[Tn * Bd:]
            outs["convs"].append(a_pad.reshape(cw - 1, Bd, D).transpose(1, 0, 2))

        xp, xp_bf, g_tail = _ffn(gp, xp, xp_bf, i, ffn_hist_p, w_ffn_gate, w_ffn_up, w_ffn_conv,
                                 b_ffn_conv, w_ffn_down, ln_ffn_g, ln_ffn_b, alpha, tf)
        outs["ffnp"].append(g_tail.reshape(B, SUBLANES, F)[:, SUBLANES - 2:])
        xs, xs_bf, g_tail = _ffn(gs, xs, xs_bf, i, ffn_hist_s[i], w_ffn_gate, w_ffn_up, w_ffn_conv,
                                 b_ffn_conv, w_ffn_down, ln_ffn_g, ln_ffn_b, alpha, tf)
        outs["ffns"].append(g_tail.reshape(2, Bd, F).transpose(1, 0, 2))

    st = lambda name: jnp.stack(outs[name])
    y_prompt = xp.reshape(B, T, D)
    y_sample = xs.reshape(Tn, Bd, D).transpose(1, 0, 2)
    return (y_prompt, y_sample, st("kp"), st("vp"), st("kip"), st("convp"), st("ffnp"),
            st("ks"), st("vs"), st("kis"), st("convs"), st("ffns"))
```

```python
import functools

import jax
import jax.numpy as jnp
from jax import lax
from jax.experimental import pallas as pl
from jax.experimental.pallas import tpu as pltpu

F32 = jnp.float32
BF16 = jnp.bfloat16
I32 = jnp.int32

N_HEADS = 16
N_KV_HEADS = 4
HEAD_DIM = 128
GROUP = N_HEADS // N_KV_HEADS
ROT_DIM = HEAD_DIM // 4
ROPE_THETA = 500000.0
N_IDX_HEADS = 16
IDX_DIM = 64
IDX_ROT_DIM = IDX_DIM // 4
IDX_W_SCALE = (N_IDX_HEADS * IDX_DIM) ** -0.5
TOPK_MAX = 256
Q_BLOCK = 128
LN_EPS = 1e-5
LANES = 128
SUBLANES = 8
PAGES_PER_STEP = 8
CAUSAL_VARIANTS = 8
T_PAD = 8
ONES_ROWS = 16
NEG_BIG = -1e30
INT_MIN = -(2 ** 31)
INT_MAX = 2 ** 31 - 1
NEG_INF_KEY = INT_MIN + 0x007FFFFF
VMEM_LIMIT = 56 * 2 ** 20


def _params(*sem):
    return pltpu.CompilerParams(dimension_semantics=sem, vmem_limit_bytes=VMEM_LIMIT)


def _dot(a, b):
    return jnp.dot(a, b, preferred_element_type=F32)


def _dot_nt(a, b):
    return lax.dot_general(a, b, (((1,), (1,)), ((), ())), preferred_element_type=F32)


def _layer_norm(z, g, b):
    mu = jnp.mean(z, axis=-1, keepdims=True)
    zc = z - mu
    var = jnp.mean(zc * zc, axis=-1, keepdims=True)
    return zc * lax.rsqrt(var + LN_EPS) * g + b


def _lane_tile(x, rep):
    return x if rep == 1 else jnp.concatenate([x] * rep, axis=1)


def _proj_body(*refs, shift, n_valid):
    x_ref, w_ref = refs[0], refs[1]
    if shift:
        c_ref, s1_ref, s2_ref = refs[2:5]
        outs = refs[5:]
    else:
        outs = refs[2:]
    x = x_ref[...]
    tn = w_ref.shape[0]
    sub = min(tn, 256)
    rep = sub // LANES
    for c0 in range(0, tn, sub):
        y = _dot_nt(x, w_ref[c0:c0 + sub, :].astype(BF16))
        if n_valid < tn:
            col = c0 + lax.broadcasted_iota(I32, y.shape, 1)
            y = jnp.where(col < n_valid, y, 0.0)
        if shift:
            y = (y * _lane_tile(c_ref[...], rep)
                 + pltpu.roll(y, shift, 1) * _lane_tile(s1_ref[...], rep)
                 + pltpu.roll(y, sub - shift, 1) * _lane_tile(s2_ref[...], rep))
        for o in outs:
            o[:, c0:c0 + sub] = y.astype(o.dtype)


def _proj(x_bf, wt, layer, col_off, ncols, tabs, shift, out_dtypes, tm, tn):
    M, K = x_bf.shape
    assert col_off % tn == 0 and ncols % tn == 0 and M % tm == 0
    cb = col_off // tn
    n_valid = min(tn, wt.shape[1] - col_off - (ncols - tn))
    in_specs = [pl.BlockSpec((tm, K), lambda i, j: (i, 0)),
                pl.BlockSpec((None, tn, K), lambda i, j: (layer, cb + j, 0))]
    args = [x_bf, wt]
    if shift:
        period = tabs[0].shape[0] // tm
        for t in tabs:
            in_specs.append(pl.BlockSpec((tm, LANES), lambda i, j: (i % period, 0)))
            args.append(t)
    outs = pl.pallas_call(
        functools.partial(_proj_body, shift=shift, n_valid=n_valid),
        grid=(M // tm, ncols // tn),
        in_specs=in_specs,
        out_specs=[pl.BlockSpec((tm, tn), lambda i, j: (i, j)) for _ in out_dtypes],
        out_shape=[jax.ShapeDtypeStruct((M, ncols), dt) for dt in out_dtypes],
        compiler_params=_params("parallel", "arbitrary"),
        name="proj",
    )(*args)
    return outs


def _glu_body(x_ref, wa_ref, wg_ref, ba_ref, bg_ref, a_ref):
    x = x_ref[...]
    tn = wa_ref.shape[1]
    sub = min(tn, 256)
    for c0 in range(0, tn, sub):
        cols = slice(c0, c0 + sub)
        a = _dot(x, wa_ref[:, cols].astype(BF16)) + ba_ref[:, cols]
        g = _dot(x, wg_ref[:, cols].astype(BF16)) + bg_ref[:, cols]
        a_ref[:, cols] = a * jax.nn.sigmoid(g)


def _glu(x_bf, w_pw1, b_pw1, layer, tm, tn):
    M, K = x_bf.shape
    C = w_pw1.shape[2] // 2
    nb = C // tn
    b2 = b_pw1[layer].reshape(1, 2 * C)
    return pl.pallas_call(
        _glu_body,
        grid=(M // tm, nb),
        in_specs=[pl.BlockSpec((tm, K), lambda i, j: (i, 0)),
                  pl.BlockSpec((None, K, tn), lambda i, j: (layer, 0, j)),
                  pl.BlockSpec((None, K, tn), lambda i, j: (layer, 0, nb + j)),
                  pl.BlockSpec((1, tn), lambda i, j: (0, j)),
                  pl.BlockSpec((1, tn), lambda i, j: (0, nb + j))],
        out_specs=pl.BlockSpec((tm, tn), lambda i, j: (i, j)),
        out_shape=jax.ShapeDtypeStruct((M, C), F32),
        compiler_params=_params("parallel", "arbitrary"),
        name="glu",
    )(x_bf, w_pw1, w_pw1, b2, b2)


def _dwconv_body(a_ref, hist_ref, w_ref, b_ref, g_ref, beta_ref, h_ref, ext_ref, conv_ref, sh_ref,
                 *, stride, hist_rows, width, seq_tiles, row_block):
    tm, C = a_ref.shape
    R = hist_rows + tm
    hist = hist_ref[...]
    if seq_tiles:
        hist = jnp.where(pl.program_id(0) % seq_tiles == 0, 0.0, hist)
    ext_ref[0:hist_rows, :] = hist
    ext_ref[hist_rows:R, :] = a_ref[...]
    tap_off = [hist_rows - (width - 1 - k) * stride for k in range(width)]
    residues = sorted({off % SUBLANES for off in tap_off} - {0})

    def chunk(ci, carry):
        c0 = pl.multiple_of(ci * LANES, LANES)
        wch = w_ref[:, pl.ds(c0, LANES)]
        bch = b_ref[:, pl.ds(c0, LANES)]
        for r in residues:
            sh_ref[r, 0:R - SUBLANES, :] = ext_ref[pl.ds(r, R - SUBLANES), pl.ds(c0, LANES)]
        for r0 in range(0, tm, row_block):
            acc = jnp.broadcast_to(bch, (row_block, LANES))
            for k in range(width):
                r = tap_off[k] % SUBLANES
                base = tap_off[k] - r + r0
                if r == 0:
                    tap = ext_ref[pl.ds(base, row_block), pl.ds(c0, LANES)]
                else:
                    tap = sh_ref[r, pl.ds(base, row_block), :]
                acc = acc + tap * wch[k:k + 1, :]
            conv_ref[pl.ds(r0, row_block), pl.ds(c0, LANES)] = acc
        return carry

    lax.fori_loop(0, C // LANES, chunk, 0)
    y = _layer_norm(conv_ref[...], g_ref[...], beta_ref[...])
    h_ref[...] = (y * jax.nn.sigmoid(y)).astype(h_ref.dtype)


def _dwconv(a, hist, w_dw, b_dw, ln_g, ln_b, *, tm, stride, hist_rows, seq_tiles):
    M, C = a.shape
    width = w_dw.shape[0]
    if seq_tiles:
        per = tm // hist_rows
        hist_arr = a
        hist_spec = pl.BlockSpec((hist_rows, C), lambda i: (jnp.maximum(i * per - 1, 0), 0))
    else:
        hist_arr = hist
        hist_spec = pl.BlockSpec((hist_rows, C), lambda i: (0, 0))
    vec = lambda: pl.BlockSpec((1, C), lambda i: (0, 0))
    return pl.pallas_call(
        functools.partial(_dwconv_body, stride=stride, hist_rows=hist_rows, width=width,
                          seq_tiles=seq_tiles, row_block=min(tm, 64)),
        grid=(M // tm,),
        in_specs=[pl.BlockSpec((tm, C), lambda i: (i, 0)), hist_spec,
                  pl.BlockSpec((width, C), lambda i: (0, 0)), vec(), vec(), vec()],
        out_specs=pl.BlockSpec((tm, C), lambda i: (i, 0)),
        out_shape=jax.ShapeDtypeStruct((M, C), BF16),
        scratch_shapes=[pltpu.VMEM((hist_rows + tm, C), F32), pltpu.VMEM((tm, C), F32),
                        pltpu.VMEM((SUBLANES, hist_rows + tm, LANES), F32)],
        compiler_params=_params("arbitrary"),
        name="dwconv",
    )(a, hist_arr, w_dw, b_dw.reshape(1, C), ln_g.reshape(1, C), ln_b.reshape(1, C))


def _mm_res_ln_body(a_ref, w_ref, bias_ref, res_ref, g_ref, b_ref, xo_ref, xb_ref, *, nk, alpha):
    k = pl.program_id(1)

    @pl.when(k == 0)
    def _():
        xo_ref[...] = jnp.zeros(xo_ref.shape, F32)

    a = a_ref[...]
    N = xo_ref.shape[1]
    nc = min(N, 512)
    for c0 in range(0, N, nc):
        xo_ref[:, c0:c0 + nc] += _dot(a, w_ref[:, c0:c0 + nc].astype(BF16))

    @pl.when(k == nk - 1)
    def _():
        z = alpha * res_ref[...] + (xo_ref[...] + bias_ref[...])
        out = _layer_norm(z, g_ref[...], b_ref[...])
        xo_ref[...] = out
        xb_ref[...] = out.astype(BF16)


def _mm_res_ln(a_bf, w, layer, bias, res, ln_g, ln_b, alpha, tm, tk):
    M, K = a_bf.shape
    N = w.shape[2]
    assert K % tk == 0
    nk = K // tk
    vec = lambda: pl.BlockSpec((1, N), lambda i, k: (0, 0))
    return pl.pallas_call(
        functools.partial(_mm_res_ln_body, nk=nk, alpha=alpha),
        grid=(M // tm, nk),
        in_specs=[pl.BlockSpec((tm, tk), lambda i, k: (i, k)),
                  pl.BlockSpec((None, tk, N), lambda i, k: (layer, k, 0)),
                  vec(),
                  pl.BlockSpec((tm, N), lambda i, k: (i, 0), pipeline_mode=pl.Buffered(1)),
                  vec(), vec()],
        out_specs=[pl.BlockSpec((tm, N), lambda i, j: (i, 0)),
                   pl.BlockSpec((tm, N), lambda i, j: (i, 0))],
        out_shape=[jax.ShapeDtypeStruct((M, N), F32), jax.ShapeDtypeStruct((M, N), BF16)],
        compiler_params=_params("parallel", "arbitrary"),
        name="mm_res_ln",
    )(a_bf, w, bias.reshape(1, N), res, ln_g.reshape(1, N), ln_b.reshape(1, N))


def _ffn_up_body(x_ref, wg_ref, wu_ref, hist_ref, wc_ref, bc_ref, h_ref, gt_ref, ext_ref,
                 *, stride, hist_rows, sub):
    x = x_ref[...]
    tm = x.shape[0]
    H = hist_rows
    tf = wg_ref.shape[1]
    for c0 in range(0, tf, sub):
        cols = slice(c0, c0 + sub)
        g = _dot(x, wg_ref[:, cols].astype(BF16))
        ext_ref[0:H, cols] = hist_ref[:, cols]
        ext_ref[H:H + tm, cols] = g
        gt_ref[:, cols] = g[tm - H:tm, :]
        u = _dot(x, wu_ref[:, cols].astype(BF16))
        wc = wc_ref[:, cols]
        c = (bc_ref[:, cols] + wc[2:3, :] * g
             + wc[1:2, :] * ext_ref[H - stride:H - stride + tm, cols]
             + wc[0:1, :] * ext_ref[H - 2 * stride:H - 2 * stride + tm, cols])
        h_ref[:, cols] = (c * jax.nn.sigmoid(c) * u).astype(h_ref.dtype)


def _ffn_up(x_bf, w_gate, w_up, layer, hist, w_c, b_c, *, tm, tf, stride, hist_rows):
    M, K = x_bf.shape
    F = w_gate.shape[2]
    H = hist_rows
    return pl.pallas_call(
        functools.partial(_ffn_up_body, stride=stride, hist_rows=H, sub=min(tf, 256)),
        grid=(M // tm, F // tf),
        in_specs=[pl.BlockSpec((tm, K), lambda i, j: (i, 0)),
                  pl.BlockSpec((None, K, tf), lambda i, j: (layer, 0, j)),
                  pl.BlockSpec((None, K, tf), lambda i, j: (layer, 0, j)),
                  pl.BlockSpec((H, tf), lambda i, j: (i, j)),
                  pl.BlockSpec((3, tf), lambda i, j: (0, j)),
                  pl.BlockSpec((1, tf), lambda i, j: (0, j))],
        out_specs=[pl.BlockSpec((tm, tf), lambda i, j: (i, j)),
                   pl.BlockSpec((H, tf), lambda i, j: (i, j))],
        out_shape=[jax.ShapeDtypeStruct((M, F), BF16),
                   jax.ShapeDtypeStruct((M // tm * H, F), F32)],
        scratch_shapes=[pltpu.VMEM((H + tm, tf), F32)],
        compiler_params=_params("parallel", "arbitrary"),
        name="ffn_up",
    )(x_bf, w_gate, w_up, hist, w_c, b_c.reshape(1, F))


def _order_key(score):
    bits = pltpu.bitcast(score + 0.0, I32)
    return jnp.where(bits < 0, bits ^ INT_MAX, bits)


COUNT_CHAINS = 8


def _count(mask):
    S, L = mask.shape
    assert S < 2 ** 24
    x = jnp.where(mask, 1.0, 0.0).astype(F32)
    group = COUNT_CHAINS * SUBLANES
    if S % group == 0 and S > group:
        chains = [x[c * SUBLANES:(c + 1) * SUBLANES] for c in range(COUNT_CHAINS)]
        for r in range(group, S, group):
            for c in range(COUNT_CHAINS):
                chains[c] = chains[c] + x[r + c * SUBLANES:r + (c + 1) * SUBLANES]
        while len(chains) > 1:
            chains = [chains[i] + chains[i + 1] for i in range(0, len(chains), 2)]
        x = chains[0]
    return jnp.sum(x, axis=0, keepdims=True).astype(I32)


def _select_topk(key_ref, cut_ref, k, lane_ok):
    S, L = key_ref.shape
    cnt0 = _count(key_ref[...] >= 0)
    base = jnp.where(cnt0 >= k, 0, INT_MIN).astype(I32)

    def bit_step(t, base):
        cand = base | lax.shift_left(jnp.int32(1), 30 - t)
        return jnp.where(_count(key_ref[...] >= cand) >= k, cand, base)

    thr = lax.fori_loop(0, 31, bit_step, base)
    key = key_ref[...]
    n_gt = _count(key > thr)
    n_ge = _count(key >= thr)
    need = k - n_gt
    tie = (n_ge > k) & (thr > NEG_INF_KEY) & lane_ok
    cut_ref[...] = jnp.full((1, L), INT_MAX, I32)
    n_tie = jnp.sum(jnp.where(tie, 1, 0).astype(I32))

    @pl.when(n_tie > 0)
    def _():
        nbits = max(1, (S - 1).bit_length())
        rows = lax.broadcasted_iota(I32, (S, L), 0)

        def idx_step(t, pos):
            cand = pos | lax.shift_left(jnp.int32(1), nbits - 1 - t)
            c = _count((key_ref[...] == thr) & (rows < cand))
            return jnp.where(c < need, cand, pos)

        cut_ref[...] = lax.fori_loop(0, nbits, idx_step, jnp.zeros((1, L), I32))

    return thr


def _attn_prompt_body(qi_ref, wit_ref, kiwi_ref, q_ref, k_ref, v_ref, o_ref,
                      ki2_ref, sc_ref, key_ref, cut_ref, vt_ref, *, k_sel, n_var):
    S = kiwi_ref.shape[0]
    tq = qi_ref.shape[0]
    qb = pl.program_id(1)

    @pl.when(qb == 0)
    def _():
        lane = lax.broadcasted_iota(I32, (S, LANES), 1)
        lo = jnp.where(lane < IDX_DIM, kiwi_ref[...], 0.0)
        ki2_ref[0:S, :] = lo.astype(BF16)
        ki2_ref[S:2 * S, :] = pltpu.roll(lo, IDX_DIM, 1).astype(BF16)
        for n in range(N_KV_HEADS):
            vt_ref[n, 0:HEAD_DIM, :] = v_ref[:, n * HEAD_DIM:(n + 1) * HEAD_DIM].T.astype(BF16)
            vt_ref[n, HEAD_DIM:HEAD_DIM + ONES_ROWS, :] = jnp.ones((ONES_ROWS, S), BF16)

    ws = wit_ref[...] * IDX_W_SCALE
    scale = HEAD_DIM ** -0.5

    def attend(su):
        sc = sc_ref.at[0:su]
        keys = key_ref.at[0:su]
        for p in range(N_IDX_HEADS // 2):
            qp = qi_ref[:, p * LANES:(p + 1) * LANES]
            d0 = _dot_nt(ki2_ref[0:su, :], qp)
            d1 = _dot_nt(ki2_ref[S:S + su, :], qp)
            part = (ws[2 * p:2 * p + 1, :] * jnp.maximum(d0, 0.0)
                    + ws[2 * p + 1:2 * p + 2, :] * jnp.maximum(d1, 0.0))
            if p == 0:
                sc[...] = part
            else:
                sc[...] += part

        rows = lax.broadcasted_iota(I32, (su, tq), 0)
        qpos = qb * tq + lax.broadcasted_iota(I32, (su, tq), 1)
        causal = rows <= qpos
        keys[...] = _order_key(jnp.where(causal, sc[...], -jnp.inf))
        thr = _select_topk(keys, cut_ref, k_sel, jnp.full((1, tq), True))
        key = keys[...]
        sel = ((key > thr) | ((key == thr) & (rows <= cut_ref[...]))) & causal
        bias = _lane_tile(jnp.where(sel, 0.0, NEG_BIG), GROUP)

        for n in range(N_KV_HEADS):
            qs = jnp.concatenate([q_ref[:, (n * GROUP + g) * HEAD_DIM:(n * GROUP + g + 1) * HEAD_DIM]
                                  for g in range(GROUP)], axis=0)
            s = _dot_nt(k_ref[0:su, n * HEAD_DIM:(n + 1) * HEAD_DIM], qs) + bias
            m = jnp.max(s, axis=0, keepdims=True)
            e = jnp.exp((s - m) * scale).astype(BF16)
            ov = _dot(vt_ref[n, :, 0:su], e)
            ot = ov[0:HEAD_DIM, :] * (1.0 / ov[HEAD_DIM:HEAD_DIM + 1, :])
            for g in range(GROUP):
                h = n * GROUP + g
                o_ref[:, h * HEAD_DIM:(h + 1) * HEAD_DIM] = (
                    ot[:, g * tq:(g + 1) * tq].T.astype(o_ref.dtype))

    nb = S // tq
    per = nb // n_var
    for c in range(n_var):
        pl.when(qb // per == c)(functools.partial(attend, (c + 1) * per * tq))


def _attn_prompt(qi_bf, wit, kiwi, q_bf, k_bf, v_f, B, T, k_sel):
    tq = Q_BLOCK
    nb = T // tq
    n_var = next(v for v in (CAUSAL_VARIANTS, 4, 2, 1) if nb % v == 0)
    QC = N_HEADS * HEAD_DIM
    KC = N_KV_HEADS * HEAD_DIM
    QI = N_IDX_HEADS * IDX_DIM
    return pl.pallas_call(
        functools.partial(_attn_prompt_body, k_sel=k_sel, n_var=n_var),
        grid=(B, nb),
        in_specs=[pl.BlockSpec((tq, QI), lambda b, i: (b * nb + i, 0)),
                  pl.BlockSpec((N_IDX_HEADS, tq), lambda b, i: (0, b * nb + i)),
                  pl.BlockSpec((T, LANES), lambda b, i: (b, 0)),
                  pl.BlockSpec((tq, QC), lambda b, i: (b * nb + i, 0)),
                  pl.BlockSpec((T, KC), lambda b, i: (b, 0)),
                  pl.BlockSpec((T, KC), lambda b, i: (b, 0))],
        out_specs=pl.BlockSpec((tq, QC), lambda b, i: (b * nb + i, 0)),
        out_shape=jax.ShapeDtypeStruct((B * T, QC), BF16),
        scratch_shapes=[pltpu.VMEM((2 * T, LANES), BF16), pltpu.VMEM((T, tq), F32),
                        pltpu.VMEM((T, tq), I32), pltpu.VMEM((1, tq), I32),
                        pltpu.VMEM((N_KV_HEADS, HEAD_DIM + ONES_ROWS, T), BF16)],
        compiler_params=_params("parallel", "arbitrary"),
        name="attn_prompt",
    )(qi_bf, wit, kiwi, q_bf, k_bf, v_f)


def _page_scores(kit, qi, w_col):
    d = _dot(qi, kit.astype(BF16))
    z = jnp.maximum(d, 0.0) * w_col
    return z.reshape(N_IDX_HEADS, T_PAD, LANES).sum(axis=0)


def _samp_score_body(pt_ref, *refs, n_tok):
    G = PAGES_PER_STEP
    pages = refs[:G]
    kinew_ref, qi_ref, wc_ref, out_ref = refs[G:G + 4]
    p = pl.program_id(1)
    n_steps = pl.num_programs(1)
    qi = qi_ref[...]
    wc = wc_ref[...]

    @pl.when(p < n_steps - 1)
    def _():
        for g in range(G):
            out_ref[:, g * LANES:(g + 1) * LANES] = _page_scores(pages[g][...], qi, wc)

    @pl.when(p == n_steps - 1)
    def _():
        z = _page_scores(kinew_ref[...], qi, wc)
        tok = lax.broadcasted_iota(I32, (T_PAD, LANES), 0)
        new = lax.broadcasted_iota(I32, (T_PAD, LANES), 1)
        ok = (new <= tok) & (new < n_tok)
        out_ref[:, 0:LANES] = jnp.where(ok, z, -jnp.inf)
        out_ref[:, LANES:G * LANES] = jnp.full((T_PAD, (G - 1) * LANES), -jnp.inf, F32)


def _page_index_map(g, layer_off, n_steps, rank=3):
    def index_map(b, p, pt):
        pg = jnp.minimum(p, n_steps - 2) * PAGES_PER_STEP + g % PAGES_PER_STEP
        return (layer_off + pt[b, pg],) + (0,) * (rank - 1)
    return index_map


def _samp_score(page_table, cache_kit, layer_off, kitnew, qi_rows, w_col, n_tok):
    Bd, n_pages = page_table.shape
    G = PAGES_PER_STEP
    n_steps = n_pages // G + 1
    page = cache_kit.shape[2]
    in_specs = [pl.BlockSpec((None, IDX_DIM, page), _page_index_map(g, layer_off, n_steps))
                for g in range(G)]
    in_specs += [pl.BlockSpec((None, IDX_DIM, LANES), lambda b, p, pt: (b, 0, 0)),
                 pl.BlockSpec((None, LANES, IDX_DIM), lambda b, p, pt: (b, 0, 0)),
                 pl.BlockSpec((None, LANES, 1), lambda b, p, pt: (b, 0, 0))]
    return pl.pallas_call(
        functools.partial(_samp_score_body, n_tok=n_tok),
        grid_spec=pltpu.PrefetchScalarGridSpec(
            num_scalar_prefetch=1, grid=(Bd, n_steps), in_specs=in_specs,
            out_specs=pl.BlockSpec((None, T_PAD, G * page), lambda b, p, pt: (b, 0, p))),
        out_shape=jax.ShapeDtypeStruct((Bd, T_PAD, n_steps * G * page), F32),
        compiler_params=_params("parallel", "arbitrary"),
        name="samp_score",
    )(page_table, *([cache_kit] * G), kitnew, qi_rows, w_col)


def _samp_select_body(sc_ref, sel_ref, key_ref, cut_ref, *, k_sel, n_lanes, n_tok):
    S = sc_ref.shape[0]
    key_ref[...] = _order_key(sc_ref[...])
    lane = lax.broadcasted_iota(I32, (1, LANES), 1)
    lane_ok = (lane < n_lanes) & (lane % T_PAD < n_tok)
    thr = _select_topk(key_ref, cut_ref, k_sel, lane_ok)
    key = key_ref[...]
    rows = lax.broadcasted_iota(I32, (S, LANES), 0)
    sel = (key > thr) | ((key == thr) & (rows <= cut_ref[...]))
    sel_ref[...] = jnp.where(sel, 1.0, 0.0).astype(sel_ref.dtype)


def _samp_select(score_t, k_sel, n_lanes, n_tok):
    S = score_t.shape[0]
    return pl.pallas_call(
        functools.partial(_samp_select_body, k_sel=k_sel, n_lanes=n_lanes, n_tok=n_tok),
        out_shape=jax.ShapeDtypeStruct((S, LANES), BF16),
        scratch_shapes=[pltpu.VMEM((S, LANES), I32), pltpu.VMEM((1, LANES), I32)],
        compiler_params=pltpu.CompilerParams(vmem_limit_bytes=VMEM_LIMIT),
        name="samp_select",
    )(score_t)


def _samp_attend_body(pt_ref, *refs):
    G = PAGES_PER_STEP
    kp = refs[:G]
    vp = refs[G:2 * G]
    (knew_ref, vnew_ref, sel_ref, pick_ref, qcat_ref, o_ref,
     m_ref, l_ref, acc_ref) = refs[2 * G:]
    p = pl.program_id(1)
    n_steps = pl.num_programs(1)
    scale = HEAD_DIM ** -0.5
    R = knew_ref.shape[0]

    @pl.when(p == 0)
    def _():
        m_ref[...] = jnp.full(m_ref.shape, NEG_BIG, F32)
        l_ref[...] = jnp.zeros(l_ref.shape, F32)
        acc_ref[...] = jnp.zeros(acc_ref.shape, F32)

    qcat = qcat_ref[...]
    pick = pick_ref[...]
    rr = lax.broadcasted_iota(I32, (R, LANES), 0)
    cc = lax.broadcasted_iota(I32, (R, LANES), 1)
    head_bias = jnp.where(rr % N_KV_HEADS == cc // (GROUP * T_PAD), 0.0, NEG_BIG)
    rep = jnp.where(rr // N_KV_HEADS == cc, 1.0, 0.0).astype(BF16)

    chosen = _dot(sel_ref[...], pick)
    bias_all = ((chosen - 1.0) * (-NEG_BIG / scale)).astype(BF16)

    def page_logits(k_ref, g):
        lhs = jnp.concatenate([k_ref[...].astype(BF16), rep], axis=1)
        rhs = jnp.concatenate([qcat, bias_all[g * LANES:(g + 1) * LANES, :]], axis=0)
        return _dot(lhs, rhs) * scale + head_bias

    def fold(k_refs, v_refs):
        logits = [page_logits(k_ref, g) for g, k_ref in enumerate(k_refs)]
        m_old = m_ref[...]
        m_new = m_old
        for s in logits:
            m_new = jnp.maximum(m_new, jnp.max(s, axis=0, keepdims=True))
        corr = jnp.exp(m_old - m_new)
        l_new = l_ref[...] * corr
        acc = acc_ref[...] * corr
        for s, v_ref in zip(logits, v_refs):
            e = jnp.exp(s - m_new)
            l_new = l_new + jnp.sum(e, axis=0, keepdims=True)
            acc = acc + _dot(v_ref[...].T.astype(BF16), e.astype(BF16))
        m_ref[...] = m_new
        l_ref[...] = l_new
        acc_ref[...] = acc

    @pl.when(p < n_steps - 1)
    def _():
        fold(kp, vp)

    @pl.when(p == n_steps - 1)
    def _():
        fold([knew_ref], [vnew_ref])
        o_ref[...] = acc_ref[...] * (1.0 / l_ref[...])


def _samp_attend(page_table, cache_k, cache_v, layer_off, knew, vnew, sel_mask, pick, qcat):
    Bd, n_pages = page_table.shape
    G = PAGES_PER_STEP
    n_steps = n_pages // G + 1
    R = cache_k.shape[1]
    page = R // N_KV_HEADS
    in_specs = [pl.BlockSpec((None, R, HEAD_DIM), _page_index_map(g, layer_off, n_steps))
                for g in range(2 * G)]
    per_seq = lambda r, c: pl.BlockSpec((None, r, c), lambda b, p, pt: (b, 0, 0))
    in_specs += [per_seq(R, HEAD_DIM), per_seq(R, HEAD_DIM),
                 pl.BlockSpec((G * page, LANES), lambda b, p, pt: (p, 0)),
                 per_seq(LANES, LANES), per_seq(HEAD_DIM, LANES)]
    return pl.pallas_call(
        _samp_attend_body,
        grid_spec=pltpu.PrefetchScalarGridSpec(
            num_scalar_prefetch=1, grid=(Bd, n_steps), in_specs=in_specs,
            out_specs=pl.BlockSpec((None, HEAD_DIM, LANES), lambda b, p, pt: (b, 0, 0)),
            scratch_shapes=[pltpu.VMEM((1, LANES), F32), pltpu.VMEM((1, LANES), F32),
                            pltpu.VMEM((HEAD_DIM, LANES), F32)]),
        out_shape=jax.ShapeDtypeStruct((Bd, HEAD_DIM, LANES), F32),
        compiler_params=_params("parallel", "arbitrary"),
        name="samp_attend",
    )(page_table, *([cache_k] * G), *([cache_v] * G), knew, vnew, sel_mask, pick, qcat)


def _rope_tables(pos, rot_dim, period):
    half = rot_dim // 2
    inv = ROPE_THETA ** (-jnp.arange(0, rot_dim, 2, dtype=F32) / rot_dim)
    ang = pos.astype(F32)[:, None] * inv[None, :]
    cos, sin = jnp.cos(ang), jnp.sin(ang)
    lane = jnp.arange(LANES) % period
    fi = lane % half
    c = jnp.where(lane[None, :] < rot_dim, cos[:, fi], 1.0)
    s_hi = jnp.where((lane[None, :] >= half) & (lane[None, :] < rot_dim), sin[:, fi], 0.0)
    s_lo = jnp.where(lane[None, :] < half, -sin[:, fi], 0.0)
    return c, s_hi, s_lo


class _Group:
    def __init__(self, rows, seq_rows, stride, pos, tm, tm_down):
        self.rows = rows
        self.seq_rows = seq_rows
        self.stride = stride
        self.tm = tm
        self.tm_down = tm_down
        self.tab_q = _rope_tables(pos, ROT_DIM, HEAD_DIM)
        self.tab_i = _rope_tables(pos, IDX_ROT_DIM, IDX_DIM)


def _in_proj(grp, x_bf, wt_in, j):
    QC = N_HEADS * HEAD_DIM
    KC = N_KV_HEADS * HEAD_DIM
    QI = N_IDX_HEADS * IDX_DIM
    tm = grp.tm
    (q_bf,) = _proj(x_bf, wt_in, j, 0, QC, grp.tab_q, ROT_DIM // 2, [BF16], tm, 512)
    k_f, k_bf = _proj(x_bf, wt_in, j, QC, KC, grp.tab_q, ROT_DIM // 2, [F32, BF16], tm, 512)
    (v_f,) = _proj(x_bf, wt_in, j, QC + KC, KC, None, 0, [F32], tm, 512)
    (qi_bf,) = _proj(x_bf, wt_in, j, QC + 2 * KC, QI, grp.tab_i, IDX_ROT_DIM // 2, [BF16], tm, 512)
    c, s_hi, s_lo = grp.tab_i
    lane = jnp.arange(LANES)[None, :]
    keep = lane < IDX_DIM
    tab_kw = (jnp.where(keep, c, 1.0), jnp.where(keep, s_hi, 0.0), jnp.where(keep, s_lo, 0.0))
    (kiwi,) = _proj(x_bf, wt_in, j, QC + 2 * KC + QI, LANES, tab_kw, IDX_ROT_DIM // 2, [F32], tm, LANES)
    return q_bf, k_f, k_bf, v_f, qi_bf, kiwi


def _ffn(grp, x, x_bf, i, hist, w_gate, w_up, w_c, b_c, w_down, ln_g, ln_b, alpha, tf):
    F = w_gate.shape[2]
    H = hist.shape[0] // (grp.rows // grp.seq_rows)
    h_bf, g_tail = _ffn_up(x_bf, w_gate, w_up, i, hist, w_c[i], b_c[i],
                           tm=grp.seq_rows, tf=tf, stride=grp.stride, hist_rows=H)
    x, x_bf = _mm_res_ln(h_bf, w_down, i, jnp.zeros((w_down.shape[2],), F32), x, ln_g[i], ln_b[i],
                         alpha, grp.tm_down, 512)
    return x, x_bf, g_tail


def kernel(x_prompt, x_sample, cache_k, cache_v, cache_kidx, state_conv, state_ffn, page_table,
           w_attn_in, w_attn_out, w_pw1, b_pw1, w_dw, b_dw, ln_conv_g, ln_conv_b, w_pw2, b_pw2,
           w_ffn_gate, w_ffn_up, w_ffn_conv, b_ffn_conv, w_ffn_down,
           ln_mix_g, ln_mix_b, ln_ffn_g, ln_ffn_b):
    B, T, D = x_prompt.shape
    Bd, Tn, _ = x_sample.shape
    depth = w_ffn_gate.shape[0]
    F = w_ffn_gate.shape[2]
    n_attn, n_pool, page = cache_k.shape[:3]
    n_conv = w_pw1.shape[0]
    cw = w_dw.shape[1]
    past = page_table.shape[1] * page
    KC = N_KV_HEADS * HEAD_DIM
    alpha = float((2 * depth) ** 0.25)
    assert Bd == SUBLANES and Tn <= T_PAD and D == N_HEADS * HEAD_DIM
    tf = 512 if F % 512 == 0 else LANES

    gp = _Group(B * T, T, 1, jnp.arange(T, dtype=I32), min(1024, T), min(1024, T))
    pos_s = past + jnp.arange(Tn, dtype=I32)
    gs = _Group(Tn * Bd, Tn * Bd, Bd, jnp.repeat(pos_s, Bd), Tn * Bd, Tn * Bd)

    xp = x_prompt.reshape(B * T, D)
    xs = x_sample.transpose(1, 0, 2).reshape(Tn * Bd, D)
    xp_bf, xs_bf = xp.astype(BF16), xs.astype(BF16)

    ck = cache_k.reshape(n_attn * n_pool, page * N_KV_HEADS, HEAD_DIM)
    cv = cache_v.reshape(n_attn * n_pool, page * N_KV_HEADS, HEAD_DIM)
    ckit = cache_kidx.reshape(n_attn * n_pool, page, IDX_DIM).transpose(0, 2, 1)
    conv_hist_s = state_conv.transpose(0, 2, 1, 3).reshape(n_conv, (cw - 1) * Bd, D)
    ffn_hist_s = state_ffn.transpose(0, 2, 1, 3).reshape(depth, 2 * Bd, F)
    ffn_hist_p = jnp.zeros((B * SUBLANES, F), F32)
    zero_d = jnp.zeros((D,), F32)

    wt_in = w_attn_in.transpose(0, 2, 1)

    src = jnp.arange(LANES)
    pick = ((src[None, :, None] // T_PAD == jnp.arange(Bd)[:, None, None])
            & (src[None, :, None] % T_PAD == src[None, None, :] % T_PAD)).astype(BF16)

    k_sel_p = min(TOPK_MAX, T // 4)
    k_sel_s = min(TOPK_MAX, (past + Tn) // 4)

    outs = {name: [] for name in ("kp", "vp", "kip", "convp", "ffnp", "ks", "vs", "kis", "convs", "ffns")}

    for i in range(depth):
        j = i // 2
        if i % 2 == 0:
            q_bf, k_f, k_bf, v_f, qi_bf, kiwi = _in_proj(gp, xp_bf, wt_in, j)
            wit = kiwi[:, IDX_DIM:IDX_DIM + N_IDX_HEADS].T
            o_bf = _attn_prompt(qi_bf, wit, kiwi, q_bf, k_bf, v_f, B, T, k_sel_p)
            xp, xp_bf = _mm_res_ln(o_bf, w_attn_out, j, zero_d, xp, ln_mix_g[i], ln_mix_b[i],
                                   alpha, gp.tm_down, 512)
            outs["kp"].append(k_f.reshape(B, T, N_KV_HEADS, HEAD_DIM))
            outs["vp"].append(v_f.reshape(B, T, N_KV_HEADS, HEAD_DIM))
            outs["kip"].append(kiwi[:, :IDX_DIM].reshape(B, T, IDX_DIM))

            q_bf, k_f, k_bf, v_f, qi_bf, kiwi = _in_proj(gs, xs_bf, wt_in, j)
            tb = lambda a: a.reshape(Tn, Bd, -1).transpose(1, 0, 2)
            pad_rows = lambda a: jnp.pad(a, ((0, 0), (0, LANES - Tn), (0, 0)))
            kitnew = pad_rows(tb(kiwi[:, :IDX_DIM])).transpose(0, 2, 1)
            heads = lambda a: a.reshape(Bd, LANES * N_KV_HEADS, HEAD_DIM)
            knew, vnew = heads(pad_rows(tb(k_f))), heads(pad_rows(tb(v_f)))
            qi4 = tb(qi_bf).reshape(Bd, Tn, N_IDX_HEADS, IDX_DIM)
            qi4 = jnp.pad(qi4, ((0, 0), (0, T_PAD - Tn), (0, 0), (0, 0)))
            qi_rows = qi4.transpose(0, 2, 1, 3).reshape(Bd, N_IDX_HEADS * T_PAD, IDX_DIM)
            wi4 = tb(kiwi[:, IDX_DIM:IDX_DIM + N_IDX_HEADS]) * IDX_W_SCALE
            wi4 = jnp.pad(wi4, ((0, 0), (0, T_PAD - Tn), (0, 0)))
            w_col = wi4.transpose(0, 2, 1).reshape(Bd, N_IDX_HEADS * T_PAD, 1)
            score8 = _samp_score(page_table, ckit, j * n_pool, kitnew, qi_rows, w_col, Tn)
            s_tot = score8.shape[2]
            score_t = score8.transpose(2, 0, 1).reshape(s_tot, Bd * T_PAD)
            score_t = jnp.pad(score_t, ((0, 0), (0, LANES - Bd * T_PAD)))
            sel_mask = _samp_select(score_t, k_sel_s, Bd * T_PAD, Tn)
            q4 = tb(q_bf).reshape(Bd, Tn, N_KV_HEADS, GROUP, HEAD_DIM)
            q4 = jnp.pad(q4, ((0, 0), (0, T_PAD - Tn), (0, 0), (0, 0), (0, 0)))
            qcat = q4.transpose(0, 4, 2, 3, 1).reshape(Bd, HEAD_DIM, N_HEADS * T_PAD)
            ot = _samp_attend(page_table, ck, cv, j * n_pool, knew, vnew, sel_mask,
                              pick, qcat)
            o = ot.reshape(Bd, HEAD_DIM, N_HEADS, T_PAD)[..., :Tn]
            o = o.transpose(3, 0, 2, 1).reshape(Tn * Bd, N_HEADS * HEAD_DIM)
            xs, xs_bf = _mm_res_ln(o.astype(BF16), w_attn_out, j, zero_d, xs, ln_mix_g[i], ln_mix_b[i],
                                   alpha, gs.tm_down, 512)
            outs["ks"].append(tb(k_f).reshape(Bd, Tn, N_KV_HEADS, HEAD_DIM))
            outs["vs"].append(tb(v_f).reshape(Bd, Tn, N_KV_HEADS, HEAD_DIM))
            outs["kis"].append(tb(kiwi[:, :IDX_DIM]))
        else:
            a = _glu(xp_bf, w_pw1, b_pw1, j, gp.tm, 512)
            h_bf = _dwconv(a, None, w_dw[j], b_dw[j], ln_conv_g[j], ln_conv_b[j],
                           tm=min(256, T), stride=1, hist_rows=32, seq_tiles=T // min(256, T))
            xp, xp_bf = _mm_res_ln(h_bf, w_pw2, j, b_pw2[j], xp, ln_mix_g[i], ln_mix_b[i],
                                   alpha, gp.tm_down, 512)
            outs["convp"].append(a.reshape(B, T, D)[:, T - (cw - 1):])

            a = _glu(xs_bf, w_pw1, b_pw1, j, gs.tm, 512)
            h_bf = _dwconv(a, conv_hist_s[j], w_dw[j], b_dw[j], ln_conv_g[j], ln_conv_b[j],
                           tm=Tn * Bd, stride=Bd, hist_rows=(cw - 1) * Bd, seq_tiles=0)
            xs, xs_bf = _mm_res_ln(h_bf, w_pw2, j, b_pw2[j], xs, ln_mix_g[i], ln_mix_b[i],
                                   alpha, gs.tm_down, 512)
            a_pad = jnp.concatenate([conv_hist_s[j], a], axis=0)[Tn * Bd:]
            outs["convs"].append(a_pad.reshape(cw - 1, Bd, D).transpose(1, 0, 2))

        xp, xp_bf, g_tail = _ffn(gp, xp, xp_bf, i, ffn_hist_p, w_ffn_gate, w_ffn_up, w_ffn_conv,
                                 b_ffn_conv, w_ffn_down, ln_ffn_g, ln_ffn_b, alpha, tf)
        outs["ffnp"].append(g_tail.reshape(B, SUBLANES, F)[:, SUBLANES - 2:])
        xs, xs_bf, g_tail = _ffn(gs, xs, xs_bf, i, ffn_hist_s[i], w_ffn_gate, w_ffn_up, w_ffn_conv,
                                 b_ffn_conv, w_ffn_down, ln_ffn_g, ln_ffn_b, alpha, tf)
        outs["ffns"].append(g_tail.reshape(2, Bd, F).transpose(1, 0, 2))

    st = lambda name: jnp.stack(outs[name])
    y_prompt = xp.reshape(B, T, D)
    y_sample = xs.reshape(Tn, Bd, D).transpose(1, 0, 2)
    return (y_prompt, y_sample, st("kp"), st("vp"), st("kip"), st("convp"), st("ffnp"),
            st("ks"), st("vs"), st("kis"), st("convs"), st("ffns"))
```
